```python
import jax, jax.numpy as jnp
from jax import lax
import numpy as np

D_MODEL = 1024
BATCH = 2
SEQ = 16384
DEPTH = 4

CHUNK = 64
EPS = 1e-6
D_A = D_MODEL
CONV_A = 3
D_B = D_MODEL
LRU_HEADS = 16
LRU_BW = D_B // LRU_HEADS
CONV_B = 4
LRU_C = 8.0
D_C = D_MODEL
RWKV_HEAD = 64
RWKV_HEADS = D_C // RWKV_HEAD
R_W = 64
R_A = 64
R_V = 32
R_G = 128
LNX_EPS = RWKV_HEAD * 1e-5
D_FF = 4 * D_MODEL
N_BRANCH = 3
COLS_A = 3 * D_A
COLS_B = 2 * D_B
COLS_GATE = N_BRANCH * D_MODEL
COLS_C = 3 * D_C + R_W + R_A + R_G
N_IN = COLS_A + COLS_B + COLS_GATE + COLS_C

kernel_name = "hybrid_conv_rglru_rwkv7_block"


def _split(t, sizes):
    out, o = [], 0
    for s in sizes:
        out.append(t[..., o:o + s])
        o += s
    return out


def rms_norm(x, g):
    xf = x.astype(jnp.float32)
    y = xf * lax.rsqrt(jnp.mean(xf * xf, axis=-1, keepdims=True) + EPS)
    return (y * g.astype(jnp.float32)).astype(x.dtype)


def causal_dwconv(x, w, b=None):
    K = w.shape[0]
    S = x.shape[1]
    xp = jnp.pad(x, ((0, 0), (K - 1, 0), (0, 0)))
    y = xp[:, K - 1:K - 1 + S] * w[K - 1]
    for j in range(K - 1):
        y = y + xp[:, j:j + S] * w[j]
    return y if b is None else y + b


def token_shift(p, mu):
    prev = jnp.pad(p, ((0, 0), (1, 0), (0, 0)))[:, :-1]
    return p + (prev - p) * mu


def rg_lru(x, w_a, b_a, w_i, b_i, a_param):
    Bsz, S, C = x.shape
    f32 = jnp.float32
    xb = x.reshape(Bsz, S, LRU_HEADS, LRU_BW)
    gate_a = jax.nn.sigmoid((jnp.einsum('bshi,hij->bshj', xb, w_a).reshape(Bsz, S, C) + b_a).astype(f32))
    gate_i = jax.nn.sigmoid((jnp.einsum('bshi,hij->bshj', xb, w_i).reshape(Bsz, S, C) + b_i).astype(f32))
    log_a = -LRU_C * gate_a * jax.nn.softplus(a_param.astype(f32))
    a = jnp.exp(log_a)
    mult = jnp.sqrt(-jnp.expm1(2.0 * log_a))
    mult = jnp.where(jnp.arange(S)[None, :, None] == 0, 1.0, mult)
    u = x.astype(f32) * gate_i * mult

    def combine(left, right):
        a_l, u_l = left
        a_r, u_r = right
        return a_l * a_r, a_r * u_l + u_r

    _, h = lax.associative_scan(combine, (a, u), axis=1)
    return h.astype(x.dtype)


def rwkv7_recurrence(r, decay, k, v, kk, a):
    Bsz, S, H, N = r.shape
    nc = S // CHUNK

    def to_chunks(t):
        return t.reshape(Bsz, nc, CHUNK, H, N).transpose(1, 2, 0, 3, 4)

    def step(state, inp):
        r_t, w_t, k_t, v_t, kk_t, a_t = inp
        sa = jnp.einsum('bhvk,bhk->bhv', state, -kk_t)
        state = (state * w_t[:, :, None, :]
                 + sa[..., None] * (kk_t * a_t)[:, :, None, :]
                 + v_t[..., None] * k_t[:, :, None, :])
        y_t = jnp.einsum('bhvk,bhk->bhv', state, r_t)
        return state, y_t

    def chunk_step(state, chunk_inp):
        return lax.scan(step, state, chunk_inp)

    state0 = jnp.zeros((Bsz, H, N, N), jnp.float32)
    inputs = (to_chunks(r), to_chunks(decay), to_chunks(k), to_chunks(v), to_chunks(kk), to_chunks(a))
    _, y = lax.scan(chunk_step, state0, inputs)
    return y.transpose(2, 0, 1, 3, 4).reshape(Bsz, S, H, N)


def rwkv7_mix(pc, h, v_first, w0, w2, a0, a2, g2, k_k, k_a, r_k, lnx_g, lnx_b, vres):
    f32 = jnp.float32
    r, k, v, xw, xa, xg = _split(pc, (D_C, D_C, D_C, R_W, R_A, R_G))
    w_log = -jax.nn.softplus(-(w0 + jnp.tanh(xw) @ w2).astype(f32)) - 0.5
    decay = jnp.exp(-jnp.exp(w_log))
    if vres is None:
        v_first = v
    else:
        v0, v1, v2 = vres
        v = v + (v_first - v) * jax.nn.sigmoid(v0 + (h @ v1) @ v2)
    a = jax.nn.sigmoid(a0 + xa @ a2)
    g = jax.nn.sigmoid(xg) @ g2

    def heads(t):
        return t.reshape(t.shape[:-1] + (RWKV_HEADS, RWKV_HEAD)).astype(f32)

    kk = heads(k * k_k)
    kk = kk / jnp.maximum(jnp.sqrt(jnp.sum(kk * kk, axis=-1, keepdims=True)), 1e-12)
    k = k * (1.0 + (a - 1.0) * k_a)
    rh, kh, vh, ah = heads(r), heads(k), heads(v), heads(a)
    y = rwkv7_recurrence(rh, heads(decay), kh, vh, kk, ah)
    mu = jnp.mean(y, axis=-1, keepdims=True)
    var = jnp.mean(jnp.square(y - mu), axis=-1, keepdims=True)
    y = (y - mu) * lax.rsqrt(var + LNX_EPS)
    y = y + jnp.sum(rh * kh * r_k.astype(f32), axis=-1, keepdims=True) * vh
    y = y.reshape(y.shape[:2] + (D_C,))
    return y, g, v_first


def setup_inputs(seed: int = 0) -> dict:
    key = jax.random.key(seed)
    ks = iter(jax.random.split(key, 48))
    L = DEPTH

    def nrm(shape, scale):
        return jax.random.normal(next(ks), shape, jnp.float32) * scale

    def uni(shape, lo, hi):
        return jax.random.uniform(next(ks), shape, jnp.float32, lo, hi)

    rad = uni((L, D_B), 0.9, 0.999)
    lru_a_param = jnp.log(jnp.expm1(-jnp.log(rad)))
    LV = max(L - 1, 0)
    return {
        "x": nrm((BATCH, SEQ, D_MODEL), 1.0),
        "norm1_g": 1.0 + nrm((L, D_MODEL), 0.05),
        "w_in": nrm((L, D_MODEL, N_IN), D_MODEL ** -0.5),
        "merge_b": nrm((L, COLS_GATE), 0.01),
        "conv_a_w": nrm((L, CONV_A, D_A), CONV_A ** -0.5),
        "lru_conv_w": nrm((L, CONV_B, D_B), CONV_B ** -0.5),
        "lru_conv_b": nrm((L, D_B), 0.01),
        "lru_wa": nrm((L, LRU_HEADS, LRU_BW, LRU_BW), LRU_BW ** -0.5),
        "lru_ba": nrm((L, D_B), 0.01),
        "lru_wi": nrm((L, LRU_HEADS, LRU_BW, LRU_BW), LRU_BW ** -0.5),
        "lru_bi": nrm((L, D_B), 0.01),
        "lru_a_param": lru_a_param,
        "rwkv_mu": uni((L, COLS_C), 0.0, 1.0),
        "rwkv_w0": uni((L, D_C), -6.0, 1.0),
        "rwkv_w2": nrm((L, R_W, D_C), 0.1 * R_W ** -0.5),
        "rwkv_a0": nrm((L, D_C), 0.1),
        "rwkv_a2": nrm((L, R_A, D_C), 0.1 * R_A ** -0.5),
        "rwkv_g2": nrm((L, R_G, D_C), R_G ** -0.5),
        "rwkv_kk": 0.85 + nrm((L, D_C), 0.05),
        "rwkv_ka": 1.0 + nrm((L, D_C), 0.05),
        "rwkv_rk": nrm((L, RWKV_HEADS, RWKV_HEAD), 0.1),
        "rwkv_lnx_g": 1.0 + nrm((L, D_C), 0.05),
        "rwkv_lnx_b": nrm((L, D_C), 0.01),
        "rwkv_v0": 1.0 + nrm((LV, D_C), 0.1),
        "rwkv_v1": nrm((LV, D_MODEL, R_V), D_MODEL ** -0.5),
        "rwkv_v2": nrm((LV, R_V, D_C), 0.1 * R_V ** -0.5),
        "w_out": nrm((L, D_MODEL, D_MODEL), D_MODEL ** -0.5),
        "norm2_g": 1.0 + nrm((L, D_MODEL), 0.05),
        "mlp_w1": nrm((L, D_MODEL, D_FF), D_MODEL ** -0.5),
        "mlp_w2": nrm((L, D_FF, D_MODEL), D_FF ** -0.5),
        "final_g": 1.0 + nrm((D_MODEL,), 0.05),
    }


def reference(x, norm1_g, w_in, merge_b, conv_a_w, lru_conv_w, lru_conv_b, lru_wa, lru_ba, lru_wi,
              lru_bi, lru_a_param, rwkv_mu, rwkv_w0, rwkv_w2, rwkv_a0, rwkv_a2, rwkv_g2, rwkv_kk,
              rwkv_ka, rwkv_rk, rwkv_lnx_g, rwkv_lnx_b, rwkv_v0, rwkv_v1, rwkv_v2, w_out, norm2_g,
              mlp_w1, mlp_w2, final_g):
    v_first = None
    for l in range(DEPTH):
        h = rms_norm(x, norm1_g[l])
        p = h @ w_in[l]
        pa, pb, pg, pc = _split(p, (COLS_A, COLS_B, COLS_GATE, COLS_C))

        b_a, c_a, x_a = _split(pa, (D_A, D_A, D_A))
        y_a = b_a * causal_dwconv(c_a * x_a, conv_a_w[l])

        x_b, g_b = _split(pb, (D_B, D_B))
        u = causal_dwconv(x_b, lru_conv_w[l], lru_conv_b[l])
        y_b = rg_lru(u, lru_wa[l], lru_ba[l], lru_wi[l], lru_bi[l], lru_a_param[l]) * jax.nn.gelu(g_b, approximate=True)

        pc = token_shift(pc, rwkv_mu[l])
        vres = None if l == 0 else (rwkv_v0[l - 1], rwkv_v1[l - 1], rwkv_v2[l - 1])
        f32 = jnp.float32
        r, k, v, xw, xa, xg = _split(pc, (D_C, D_C, D_C, R_W, R_A, R_G))
        w_log = -jax.nn.softplus(-(rwkv_w0[l] + jnp.tanh(xw) @ rwkv_w2[l]).astype(f32)) - 0.5
        decay = jnp.exp(-jnp.exp(w_log))
        if vres is None:
            v_first = v
        else:
            v0, v1, v2 = vres
            v = v + (v_first - v) * jax.nn.sigmoid(v0 + (h @ v1) @ v2)
        a = jax.nn.sigmoid(rwkv_a0[l] + xa @ rwkv_a2[l])
        g_c = jax.nn.sigmoid(xg) @ rwkv_g2[l]
        hs = x.shape[:2] + (RWKV_HEADS, RWKV_HEAD)
        kk = (k * rwkv_kk[l]).astype(f32).reshape(hs)
        kk = kk / jnp.maximum(jnp.sqrt(jnp.sum(kk * kk, axis=-1, keepdims=True)), 1e-12)
        k = k * (1.0 + (a - 1.0) * rwkv_ka[l])
        rh = r.astype(f32).reshape(hs)
        kh = k.astype(f32).reshape(hs)
        vh = v.astype(f32).reshape(hs)
        ah = a.astype(f32).reshape(hs)
        yc = rwkv7_recurrence(rh, decay.reshape(hs), kh, vh, kk, ah)
        mu = jnp.mean(yc, axis=-1, keepdims=True)
        var = jnp.mean(jnp.square(yc - mu), axis=-1, keepdims=True)
        yc = ((yc - mu) * lax.rsqrt(var + LNX_EPS)).reshape(x.shape[:2] + (D_C,))
        yc = yc * rwkv_lnx_g[l].astype(f32) + rwkv_lnx_b[l].astype(f32)
        bonus = jnp.sum(rh * kh * rwkv_rk[l].astype(f32), axis=-1, keepdims=True) * vh
        yc = yc + bonus.reshape(x.shape[:2] + (D_C,))
        y_c = (yc * g_c.astype(f32)).astype(x.dtype)

        gate_a, gate_b, gate_c = _split(jax.nn.sigmoid(pg + merge_b[l]), (D_MODEL, D_MODEL, D_MODEL))
        m = gate_a * y_a + gate_b * y_b + gate_c * y_c
        x = x + m @ w_out[l]

        h2 = rms_norm(x, norm2_g[l])
        x = x + jnp.square(jax.nn.relu(h2 @ mlp_w1[l])) @ mlp_w2[l]
    return rms_norm(x, final_g)
```

```python
import functools

import jax
import jax.numpy as jnp
from jax import lax
from jax.experimental import pallas as pl
from jax.experimental.pallas import tpu as pltpu

F32 = jnp.float32
BF16 = jnp.bfloat16

EPS = 1e-6
LRU_C = 8.0
LRU_HEADS = 16
HEAD = 64
CHUNK = 64
LNX_EPS = HEAD * 1e-5
R_W, R_A, R_G, R_V = 64, 64, 128, 32
LANES = 128
SUBLANES = 8
VMEM_LIMIT = 48 * 1024 * 1024


def _dot(a, b):
    return jnp.dot(a, b, preferred_element_type=F32)


def _sigmoid(x):
    return 1.0 / (1.0 + jnp.exp(-x))


def _softplus(x):
    return jnp.maximum(x, 0.0) + jnp.log1p(jnp.exp(-jnp.abs(x)))


def _rms(x, g):
    return x * lax.rsqrt(jnp.mean(x * x, axis=-1, keepdims=True) + EPS) * g


def _split3(x):
    p1 = x.astype(BF16)
    e1 = x - p1.astype(F32)
    p2 = e1.astype(BF16)
    p3 = (e1 - p2.astype(F32)).astype(BF16)
    return p1, p2, p3


def _sum_right(x, ones_bf):
    p1, p2, p3 = _split3(x)
    return _dot(p1, ones_bf) + _dot(p2, ones_bf) + _dot(p3, ones_bf)


def _sum_left(ones_bf, x):
    p1, p2, p3 = _split3(x)
    return _dot(ones_bf, p1) + _dot(ones_bf, p2) + _dot(ones_bf, p3)


def _inproj_kernel(*refs, has_v1):
    if has_v1:
        x_ref, g_ref, w_ref, v1_ref, p_ref, hv_ref, h_scr = refs
    else:
        x_ref, g_ref, w_ref, p_ref, h_scr = refs

    @pl.when(pl.program_id(1) == 0)
    def _():
        hb = _rms(x_ref[...], g_ref[...]).astype(BF16)
        h_scr[...] = hb
        if has_v1:
            hv_ref[...] = _dot(hb, v1_ref[...])

    p_ref[...] = _dot(h_scr[...], w_ref[...])


def _inproj(x2, g, w_bf, v1p_bf, tm, tn):
    T, D = x2.shape
    N = w_bf.shape[1]
    has_v1 = v1p_bf is not None
    in_specs = [pl.BlockSpec((tm, D), lambda i, j: (i, 0)),
                pl.BlockSpec((1, D), lambda i, j: (0, 0)),
                pl.BlockSpec((D, tn), lambda i, j: (0, j))]
    args = [x2, g, w_bf]
    out_shape = [jax.ShapeDtypeStruct((T, N), F32)]
    out_specs = [pl.BlockSpec((tm, tn), lambda i, j: (i, j))]
    if has_v1:
        in_specs.append(pl.BlockSpec((D, LANES), lambda i, j: (0, 0)))
        args.append(v1p_bf)
        out_shape.append(jax.ShapeDtypeStruct((T, LANES), F32))
        out_specs.append(pl.BlockSpec((tm, LANES), lambda i, j: (i, 0)))
    res = pl.pallas_call(
        functools.partial(_inproj_kernel, has_v1=has_v1),
        grid=(T // tm, N // tn),
        in_specs=in_specs, out_specs=out_specs, out_shape=out_shape,
        scratch_shapes=[pltpu.VMEM((tm, D), BF16)],
        compiler_params=pltpu.CompilerParams(
            dimension_semantics=("parallel", "arbitrary"), vmem_limit_bytes=VMEM_LIMIT),
        name="inproj",
    )(*args)
    return (res[0], res[1]) if has_v1 else (res[0], None)


def _shifted(pad_ref, x, j, ts):
    pad_ref[SUBLANES:SUBLANES + ts, :] = x
    return [pad_ref[SUBLANES - d:SUBLANES - d + ts, :] for d in range(1, j + 1)]


def _mix_ab_kernel(ba_ref, ca_ref, xa_ref, xb_ref, gb_ref, ga_ref, gbb_ref,
                   cwa_ref, cwb_ref, cbb_ref, wg_ref, bia_ref, bii_ref, ap_ref, mba_ref, mbb_ref,
                   o_ref, zpad, xpad, hcar, *, ts):
    t = pl.program_id(1)
    D = o_ref.shape[-1]

    @pl.when(t == 0)
    def _():
        zpad[0:SUBLANES, :] = jnp.zeros((SUBLANES, D), F32)
        xpad[0:SUBLANES, :] = jnp.zeros((SUBLANES, D), F32)
        hcar[...] = jnp.zeros_like(hcar)

    z = ca_ref[...] * xa_ref[...]
    z1, z2 = _shifted(zpad, z, 2, ts)
    cwa = cwa_ref[...]
    y_a = ba_ref[...] * (z * cwa[2:3] + z1 * cwa[1:2] + z2 * cwa[0:1])
    zpad[0:SUBLANES, :] = z[ts - SUBLANES:, :]

    xb = xb_ref[...]
    x1, x2, x3 = _shifted(xpad, xb, 3, ts)
    cwb = cwb_ref[...]
    u = xb * cwb[3:4] + x1 * cwb[2:3] + x2 * cwb[1:2] + x3 * cwb[0:1] + cbb_ref[...]
    xpad[0:SUBLANES, :] = xb[ts - SUBLANES:, :]

    ub = u.astype(BF16)
    pre_a, pre_i = [], []
    for q in range(D // LANES):
        res = _dot(ub[:, q * LANES:(q + 1) * LANES], wg_ref[q])
        pre_a.append(res[:, :LANES])
        pre_i.append(res[:, LANES:])
    gate_a = _sigmoid(jnp.concatenate(pre_a, axis=-1) + bia_ref[...])
    gate_i = _sigmoid(jnp.concatenate(pre_i, axis=-1) + bii_ref[...])
    log_a = (-LRU_C) * gate_a * _softplus(ap_ref[...])
    a = jnp.exp(log_a)
    mult = jnp.sqrt(-jnp.tanh(log_a) * (1.0 + a * a))
    row = lax.broadcasted_iota(jnp.int32, (ts, D), 0)
    mult = jnp.where((row == 0) & (t == 0), 1.0, mult)
    h = u * gate_i * mult

    d = 1
    while d < ts:
        keep = row >= d
        h_sh = jnp.where(keep, pltpu.roll(h, d, 0), 0.0)
        a_sh = jnp.where(keep, pltpu.roll(a, d, 0), 1.0)
        h = h + a * h_sh
        a = a * a_sh
        d *= 2
    h = h + a * hcar[...]
    hcar[...] = h[ts - 1:ts, :]

    gb = gb_ref[...]
    gelu = 0.5 * gb * (1.0 + jnp.tanh(0.7978845608028654 * (gb + 0.044715 * gb * gb * gb)))
    y_b = h * gelu
    o_ref[...] = (_sigmoid(ga_ref[...] + mba_ref[...]) * y_a
                  + _sigmoid(gbb_ref[...] + mbb_ref[...]) * y_b)


def _mix_ab(p3, cwa, cwb, cbb, wg, bia, bii, ap, mba, mbb, ts):
    B, S, _ = p3.shape
    D = cwa.shape[-1]

    def col(c):
        return pl.BlockSpec((None, ts, D), lambda b, t, c=c: (b, t, c))

    def par(shape):
        nd = len(shape)
        return pl.BlockSpec(shape, lambda b, t, nd=nd: (0,) * nd)

    return pl.pallas_call(
        functools.partial(_mix_ab_kernel, ts=ts),
        grid=(B, S // ts),
        in_specs=[col(0), col(1), col(2), col(3), col(4), col(5), col(6),
                  par(cwa.shape), par(cwb.shape), par(cbb.shape), par(wg.shape),
                  par(bia.shape), par(bii.shape), par(ap.shape), par(mba.shape), par(mbb.shape)],
        out_specs=pl.BlockSpec((None, ts, D), lambda b, t: (b, t, 0)),
        out_shape=jax.ShapeDtypeStruct((B, S, D), F32),
        scratch_shapes=[pltpu.VMEM((ts + SUBLANES, D), F32), pltpu.VMEM((ts + SUBLANES, D), F32),
                        pltpu.VMEM((1, D), F32)],
        compiler_params=pltpu.CompilerParams(
            dimension_semantics=("parallel", "arbitrary"), vmem_limit_bytes=VMEM_LIMIT),
        name="mix_ab",
    )(p3, p3, p3, p3, p3, p3, p3, cwa, cwb, cbb, wg, bia, bii, ap, mba, mbb)


def _bmm(a, b):
    return jnp.einsum('cik,ckj->cij', a.astype(BF16), b.astype(BF16), preferred_element_type=F32)


def _bmm_nt(a, b):
    return jnp.einsum('cik,cjk->cij', a.astype(BF16), b.astype(BF16), preferred_element_type=F32)


def _bmm_tn(a, b):
    return jnp.einsum('cti,ctj->cij', a.astype(BF16), b.astype(BF16), preferred_element_type=F32)


def _tri_inv(A, row, col):
    def blk(s):
        return (row // s) == (col // s)

    eye = (row == col).astype(F32)
    Ad = jnp.where(blk(SUBLANES), A, 0.0)
    T = eye + Ad
    P = Ad
    s = 2
    while s < SUBLANES:
        P = _bmm(P, P)
        T = T + _bmm(T, P)
        s *= 2
    s = SUBLANES
    while s < A.shape[-1]:
        Aoff = jnp.where(blk(2 * s) & jnp.logical_not(blk(s)), A, 0.0)
        T = T + _bmm(_bmm(T, Aoff), T)
        s *= 2
    return T


def _rwkv_kernel(*refs, ts, hg, layer0):
    it = iter(refs)
    pr_ref, pk_ref, pv_ref, pwa_ref, pxg_ref, pgc_ref, mab_ref = [next(it) for _ in range(7)]
    if not layer0:
        hv_ref, vf_ref = next(it), next(it)
    (mur_ref, muk_ref, muv_ref, muwa_ref, mug_ref, w0_ref, a0_ref, kk_ref, ka_ref, rk_ref,
     lng_ref, lnb_ref, mbc_ref) = [next(it) for _ in range(13)]
    if not layer0:
        v0_ref = next(it)
    w2_ref, a2_ref, g2_ref = next(it), next(it), next(it)
    if not layer0:
        v2_ref = next(it)
    o_ref = next(it)
    if layer0:
        vfo_ref = next(it)
    rpad, kpad, vpad, wapad, xgpad, st_scr = [next(it) for _ in range(6)]

    t = pl.program_id(2)
    cw = hg * HEAD
    nc = ts // CHUNK
    L = CHUNK

    @pl.when(t == 0)
    def _():
        for pad in (rpad, kpad, vpad, wapad, xgpad):
            pad[0:SUBLANES, :] = jnp.zeros((SUBLANES, pad.shape[-1]), F32)
        st_scr[...] = jnp.zeros_like(st_scr)

    def token_shift(pad, x_ref, mu_ref):
        x = x_ref[...]
        (prev,) = _shifted(pad, x, 1, ts)
        pad[0:SUBLANES, :] = x[ts - SUBLANES:, :]
        return x + (prev - x) * mu_ref[...]

    r = token_shift(rpad, pr_ref, mur_ref)
    k = token_shift(kpad, pk_ref, muk_ref)
    v = token_shift(vpad, pv_ref, muv_ref)
    xwa = token_shift(wapad, pwa_ref, muwa_ref)
    xg = token_shift(xgpad, pxg_ref, mug_ref)

    wl = w0_ref[...] + _dot(jnp.tanh(xwa).astype(BF16), w2_ref[...])
    ld = -jnp.exp(-_softplus(-wl) - 0.5)
    a = _sigmoid(a0_ref[...] + _dot(xwa.astype(BF16), a2_ref[...]))
    g = _dot(_sigmoid(xg).astype(BF16), g2_ref[...])
    if layer0:
        vfo_ref[...] = v
    else:
        mix = _sigmoid(v0_ref[...] + _dot(hv_ref[...].astype(BF16), v2_ref[...]))
        v = v + (vf_ref[...] - v) * mix

    ri = lax.broadcasted_iota(jnp.int32, (ts, ts), 0)
    ci = lax.broadcasted_iota(jnp.int32, (ts, ts), 1)
    same = (ri // L) == (ci // L)
    tri = jnp.where(same & (ci <= ri), 1.0, 0.0).astype(BF16)
    ones_t = jnp.where(same, 1.0, 0.0).astype(BF16)
    hi = lax.broadcasted_iota(jnp.int32, (cw, cw), 0)
    hj = lax.broadcasted_iota(jnp.int32, (cw, cw), 1)
    ones_h = jnp.where((hi // HEAD) == (hj // HEAD), 1.0, 0.0).astype(BF16)

    c = _sum_left(tri, ld)
    cl = _sum_left(ones_t, ld)
    e_c = jnp.exp(c)
    e_n = jnp.exp(-c)
    e_l = jnp.exp(cl - c)

    kk = k * kk_ref[...]
    kk = kk / jnp.maximum(jnp.sqrt(_sum_right(kk * kk, ones_h)), 1e-12)
    k = k * (1.0 + (a - 1.0) * ka_ref[...])
    b = kk * a
    Rt_f = (r * e_c).astype(BF16)
    At_f = (-kk * jnp.exp(c - ld)).astype(BF16)
    Bt_f = (b * e_n).astype(BF16)
    Kt_f = (k * e_n).astype(BF16)
    Bh_f = (b * e_l).astype(BF16)
    Kh_f = (k * e_l).astype(BF16)
    v_b = v.astype(BF16)

    lrow = lax.broadcasted_iota(jnp.int32, (L, L), 0)
    lcol = lax.broadcasted_iota(jnp.int32, (L, L), 1)
    low_incl = lcol <= lrow
    low_strict = lcol < lrow
    eye = lcol == lrow

    def head3(x, h):
        return x[:, h * HEAD:(h + 1) * HEAD].reshape(nc, L, HEAD)

    ys_heads = []
    for h in range(hg):
        Rt, At, Bt, Kt, Bh, Kh, vh = [head3(x, h) for x in (Rt_f, At_f, Bt_f, Kt_f, Bh_f, Kh_f, v_b)]
        Aab = jnp.where(low_strict, _bmm_nt(At, Bt), 0.0)
        Aak = jnp.where(low_strict, _bmm_nt(At, Kt), 0.0)
        Arb = jnp.where(low_incl, _bmm_nt(Rt, Bt), 0.0)
        Ark = jnp.where(low_incl, _bmm_nt(Rt, Kt), 0.0)
        T = _tri_inv(Aab, lrow, lcol)
        W = _bmm(T, At)
        U0 = _bmm(T, _bmm(Aak, vh))
        Q = Rt.astype(F32) + _bmm(Arb, W)
        Y0 = _bmm(Arb, U0) + _bmm(Ark, vh)
        pl_row = head3(e_c, h)[:, L - 1:L, :]
        M = jnp.where(eye, pl_row, 0.0) + _bmm_tn(Bh, W)
        G = _bmm_tn(Bh, U0) + _bmm_tn(Kh, vh)

        H = st_scr[h]
        ys = []
        for ch in range(nc):
            Hb = H.astype(BF16)
            ys.append(_dot(Q[ch].astype(BF16), Hb) + Y0[ch])
            H = _dot(M[ch].astype(BF16), Hb) + G[ch]
        st_scr[h] = H
        ys_heads.append(jnp.concatenate(ys, axis=0))

    y = jnp.concatenate(ys_heads, axis=-1)
    inv_n = 1.0 / HEAD
    yc = y - _sum_right(y, ones_h) * inv_n
    var = _sum_right(yc * yc, ones_h) * inv_n
    yn = yc * lax.rsqrt(var + LNX_EPS)
    bonus = _sum_right(r * k * rk_ref[...], ones_h)
    y_c = (yn * lng_ref[...] + lnb_ref[...] + bonus * v) * g
    m = mab_ref[...] + _sigmoid(pgc_ref[...] + mbc_ref[...]) * y_c
    o_ref[...] = m.astype(o_ref.dtype)


def _rwkv(p3, hv3, vf3, mab3, mu, w0, a0, kkp, kap, rkp, lng, lnb, mbc, v0, w2p, a2p, g2, v2p,
          col_r, col_k, col_v, col_wa, col_xg, col_gc, ts, hg):
    B, S, _ = p3.shape
    D = w0.shape[-1]
    cw = hg * HEAD
    layer0 = vf3 is None

    def colblk(c0, w, grouped):
        return pl.BlockSpec((None, ts, w), lambda b, gi, t: (b, t, c0 // w + (gi if grouped else 0)))

    def rowpar(c0, w, grouped):
        return pl.BlockSpec((1, w), lambda b, gi, t: (0, c0 // w + (gi if grouped else 0)))

    def matpar(rows):
        return pl.BlockSpec((rows, cw), lambda b, gi, t: (0, gi))

    act = pl.BlockSpec((None, ts, cw), lambda b, gi, t: (b, t, gi))
    in_specs = [colblk(col_r, cw, True), colblk(col_k, cw, True), colblk(col_v, cw, True),
                colblk(col_wa, LANES, False), colblk(col_xg, LANES, False), colblk(col_gc, cw, True), act]
    args = [p3, p3, p3, p3, p3, p3, mab3]
    if not layer0:
        in_specs += [pl.BlockSpec((None, ts, LANES), lambda b, gi, t: (b, t, 0)), act]
        args += [hv3, vf3]
    in_specs += [rowpar(0, cw, True), rowpar(D, cw, True), rowpar(2 * D, cw, True),
                 rowpar(3 * D, LANES, False), rowpar(3 * D + LANES, LANES, False)]
    args += [mu] * 5
    in_specs += [rowpar(0, cw, True)] * 8
    args += [w0, a0, kkp, kap, rkp, lng, lnb, mbc]
    if not layer0:
        in_specs.append(rowpar(0, cw, True))
        args.append(v0)
    in_specs += [matpar(LANES), matpar(LANES), matpar(LANES)]
    args += [w2p, a2p, g2]
    if not layer0:
        in_specs.append(matpar(LANES))
        args.append(v2p)
    out_shape = [jax.ShapeDtypeStruct((B, S, D), BF16)]
    out_specs = [act]
    if layer0:
        out_shape.append(jax.ShapeDtypeStruct((B, S, D), F32))
        out_specs.append(act)
    res = pl.pallas_call(
        functools.partial(_rwkv_kernel, ts=ts, hg=hg, layer0=layer0),
        grid=(B, D // cw, S // ts),
        in_specs=in_specs, out_specs=out_specs, out_shape=out_shape,
        scratch_shapes=[pltpu.VMEM((ts + SUBLANES, cw), F32)] * 3
        + [pltpu.VMEM((ts + SUBLANES, LANES), F32)] * 2
        + [pltpu.VMEM((hg, HEAD, HEAD), F32)],
        compiler_params=pltpu.CompilerParams(
            dimension_semantics=("parallel", "parallel", "arbitrary"), vmem_limit_bytes=VMEM_LIMIT),
        name="rwkv",
    )(*args)
    return (res[0], res[1]) if layer0 else (res[0], vf3)


def _mlp_kernel(x_ref, m_ref, wo_ref, g2_ref, w1_ref, w2_ref, fg_ref, o_ref, acc, h2_scr, *, final):
    j = pl.program_id(1)

    @pl.when(j == 0)
    def _():
        xn = x_ref[...] + _dot(m_ref[...], wo_ref[...])
        acc[...] = xn
        h2_scr[...] = _rms(xn, g2_ref[...]).astype(BF16)

    hid = jnp.square(jnp.maximum(_dot(h2_scr[...], w1_ref[...]), 0.0))
    acc[...] += _dot(hid.astype(BF16), w2_ref[...])

    @pl.when(j == pl.num_programs(1) - 1)
    def _():
        o_ref[...] = _rms(acc[...], fg_ref[...]) if final else acc[...]


def _mlp(x2, m2, wo_bf, g2, w1_bf, w2_bf, fg, tm, tf, final):
    T, D = x2.shape
    FF = w1_bf.shape[1]
    return pl.pallas_call(
        functools.partial(_mlp_kernel, final=final),
        grid=(T // tm, FF // tf),
        in_specs=[pl.BlockSpec((tm, D), lambda i, j: (i, 0)),
                  pl.BlockSpec((tm, D), lambda i, j: (i, 0)),
                  pl.BlockSpec((D, D), lambda i, j: (0, 0)),
                  pl.BlockSpec((1, D), lambda i, j: (0, 0)),
                  pl.BlockSpec((D, tf), lambda i, j: (0, j)),
                  pl.BlockSpec((tf, D), lambda i, j: (j, 0)),
                  pl.BlockSpec((1, D), lambda i, j: (0, 0))],
        out_specs=pl.BlockSpec((tm, D), lambda i, j: (i, 0)),
        out_shape=jax.ShapeDtypeStruct((T, D), F32),
        scratch_shapes=[pltpu.VMEM((tm, D), F32), pltpu.VMEM((tm, D), BF16)],
        compiler_params=pltpu.CompilerParams(
            dimension_semantics=("parallel", "arbitrary"), vmem_limit_bytes=VMEM_LIMIT),
        name="mlp",
    )(x2, m2, wo_bf, g2, w1_bf, w2_bf, fg)


def _tile(n, pref):
    t = min(n, pref)
    while n % t:
        t //= 2
    return t


def _col_tile(n, pref):
    best = LANES
    for m in range(1, n // LANES + 1):
        if (n // LANES) % m == 0 and m * LANES <= pref:
            best = m * LANES
    return best


def kernel(x, norm1_g, w_in, merge_b, conv_a_w, lru_conv_w, lru_conv_b, lru_wa, lru_ba, lru_wi, lru_bi,
           lru_a_param, rwkv_mu, rwkv_w0, rwkv_w2, rwkv_a0, rwkv_a2, rwkv_g2, rwkv_kk, rwkv_ka, rwkv_rk,
           rwkv_lnx_g, rwkv_lnx_b, rwkv_v0, rwkv_v1, rwkv_v2, w_out, norm2_g, mlp_w1, mlp_w2, final_g):
    B, S, D = x.shape
    depth = w_in.shape[0]
    T = B * S
    n_in = w_in.shape[-1]
    col_gate = 3 * D + 2 * D
    col_c = col_gate + 3 * D
    col_wa = col_c + 3 * D
    col_xg = col_wa + R_W + R_A
    assert n_in == col_xg + R_G and R_W + R_A == LANES and R_G == LANES and D % LANES == 0

    tm = _tile(T, 1024)
    tn = _col_tile(n_in, 1280)
    ts_ab = _tile(S, 256)
    ts_c = _tile(S, 256)
    hg = 4
    tm_mlp = _tile(T, 512)
    tf = _tile(mlp_w1.shape[-1], 1024)

    row = lambda vec: vec.reshape(1, -1)
    x2 = x.reshape(T, D)
    vf3 = None
    for l in range(depth):
        w_bf = w_in[l].astype(BF16)
        v1p = v2p = None
        if l > 0:
            v1p = jnp.pad(rwkv_v1[l - 1], ((0, 0), (0, LANES - R_V))).astype(BF16)
            v2p = jnp.pad(rwkv_v2[l - 1], ((0, LANES - R_V), (0, 0))).astype(BF16)
        w2p = jnp.pad(rwkv_w2[l], ((0, R_A), (0, 0))).astype(BF16)
        a2p = jnp.pad(rwkv_a2[l], ((R_W, 0), (0, 0))).astype(BF16)
        g2 = rwkv_g2[l].astype(BF16)
        bw = D // LRU_HEADS
        wa = lru_wa[l].reshape(LRU_HEADS // 2, 2, bw, bw)
        wi = lru_wi[l].reshape(LRU_HEADS // 2, 2, bw, bw)
        z = jnp.zeros_like(wa[:, 0])
        blockdiag = lambda w: jnp.concatenate(
            [jnp.concatenate([w[:, 0], z], axis=-1), jnp.concatenate([z, w[:, 1]], axis=-1)], axis=-2)
        wg = jnp.concatenate([blockdiag(wa), blockdiag(wi)], axis=-1).astype(BF16)
        mb = merge_b[l]

        p2, hv2 = _inproj(x2, row(norm1_g[l]), w_bf, v1p, tm, tn)
        p3 = p2.reshape(B, S, n_in)
        mab3 = _mix_ab(p3, conv_a_w[l], lru_conv_w[l], row(lru_conv_b[l]), wg, row(lru_ba[l]), row(lru_bi[l]),
                       row(lru_a_param[l]), row(mb[:D]), row(mb[D:2 * D]), ts_ab)
        m3, vf3 = _rwkv(p3, None if l == 0 else hv2.reshape(B, S, LANES), vf3, mab3,
                        row(rwkv_mu[l]), row(rwkv_w0[l]), row(rwkv_a0[l]), row(rwkv_kk[l]), row(rwkv_ka[l]),
                        row(rwkv_rk[l]), row(rwkv_lnx_g[l]), row(rwkv_lnx_b[l]), row(mb[2 * D:]),
                        None if l == 0 else row(rwkv_v0[l - 1]), w2p, a2p, g2, v2p,
                        col_c, col_c + D, col_c + 2 * D, col_wa, col_xg, col_gate + 2 * D, ts_c, hg)
        x2 = _mlp(x2, m3.reshape(T, D), w_out[l].astype(BF16), row(norm2_g[l]), mlp_w1[l].astype(BF16),
                  mlp_w2[l].astype(BF16), row(final_g), tm_mlp, tf, final=(l == depth - 1))
    return x2.reshape(B, S, D)
```

```python
import functools

import jax
import jax.numpy as jnp
from jax import lax
from jax.experimental import pallas as pl
from jax.experimental.pallas import tpu as pltpu

F32 = jnp.float32
BF16 = jnp.bfloat16

EPS = 1e-6
LRU_C = 8.0
LRU_HEADS = 16
HEAD = 64
CHUNK = 64
LNX_EPS = HEAD * 1e-5
EXP_M05 = 0.6065306597126334
R_W, R_A, R_G, R_V = 64, 64, 128, 32
LANES = 128
SUBLANES = 8
MXU_DIM = 256
VMEM_LIMIT = 48 * 1024 * 1024


def _dot(a, b):
    return jnp.dot(a, b, preferred_element_type=F32)


def _sigmoid(x):
    return 1.0 / (1.0 + jnp.exp(-x))


def _softplus(x):
    return jnp.maximum(x, 0.0) + jnp.log1p(jnp.exp(-jnp.abs(x)))


def _rms(x, g):
    return x * lax.rsqrt(jnp.mean(x * x, axis=-1, keepdims=True) + EPS) * g


def _split3(x):
    p1 = x.astype(BF16)
    e1 = x - p1.astype(F32)
    p2 = e1.astype(BF16)
    p3 = (e1 - p2.astype(F32)).astype(BF16)
    return p1, p2, p3


def _sum_right(x, ones_bf):
    p1 = x.astype(BF16)
    p2 = (x - p1.astype(F32)).astype(BF16)
    w = ones_bf.shape[0]
    cols = [_dot(p1[:, i:i + w], ones_bf) + _dot(p2[:, i:i + w], ones_bf) for i in range(0, x.shape[1], w)]
    return cols[0] if len(cols) == 1 else jnp.concatenate(cols, axis=1)


def _sum_left(ones_bf, x):
    p1, p2, p3 = _split3(x)
    return _dot(ones_bf, p1) + _dot(ones_bf, p2) + _dot(ones_bf, p3)


def _inproj_kernel(*refs, has_v1):
    if has_v1:
        x_ref, g_ref, w_ref, v1_ref, p_ref, hv_ref, h_scr = refs
    else:
        x_ref, g_ref, w_ref, p_ref, h_scr = refs

    @pl.when(pl.program_id(1) == 0)
    def _():
        hb = _rms(x_ref[...], g_ref[...]).astype(BF16)
        h_scr[...] = hb
        if has_v1:
            hv_ref[...] = _dot(hb, v1_ref[...])

    p_ref[...] = _dot(h_scr[...], w_ref[...])


def _inproj(x2, g, w_bf, v1p_bf, tm, tn):
    T, D = x2.shape
    N = w_bf.shape[1]
    has_v1 = v1p_bf is not None
    in_specs = [pl.BlockSpec((tm, D), lambda i, j: (i, 0)),
                pl.BlockSpec((1, D), lambda i, j: (0, 0)),
                pl.BlockSpec((D, tn), lambda i, j: (0, j))]
    args = [x2, g, w_bf]
    out_shape = [jax.ShapeDtypeStruct((T, N), F32)]
    out_specs = [pl.BlockSpec((tm, tn), lambda i, j: (i, j))]
    if has_v1:
        in_specs.append(pl.BlockSpec((D, LANES), lambda i, j: (0, 0)))
        args.append(v1p_bf)
        out_shape.append(jax.ShapeDtypeStruct((T, LANES), F32))
        out_specs.append(pl.BlockSpec((tm, LANES), lambda i, j: (i, 0)))
    res = pl.pallas_call(
        functools.partial(_inproj_kernel, has_v1=has_v1),
        grid=(T // tm, N // tn),
        in_specs=in_specs, out_specs=out_specs, out_shape=out_shape,
        scratch_shapes=[pltpu.VMEM((tm, D), BF16)],
        compiler_params=pltpu.CompilerParams(
            dimension_semantics=("parallel", "arbitrary"), vmem_limit_bytes=VMEM_LIMIT),
        name="inproj",
    )(*args)
    return (res[0], res[1]) if has_v1 else (res[0], None)


def _shifted(pad_ref, x, j, ts):
    pad_ref[SUBLANES:SUBLANES + ts, :] = x
    return [pad_ref[SUBLANES - d:SUBLANES - d + ts, :] for d in range(1, j + 1)]


def _mix_ab_kernel(ba_ref, ca_ref, xa_ref, xb_ref, gb_ref, ga_ref, gbb_ref,
                   cwa_ref, cwb_ref, cbb_ref, wg_ref, bia_ref, bii_ref, ap_ref, mba_ref, mbb_ref,
                   o_ref, zpad, xpad, hcar, *, ts):
    t = pl.program_id(1)
    D = o_ref.shape[-1]

    @pl.when(t == 0)
    def _():
        zpad[0:SUBLANES, :] = jnp.zeros((SUBLANES, D), F32)
        xpad[0:SUBLANES, :] = jnp.zeros((SUBLANES, D), F32)
        hcar[...] = jnp.zeros_like(hcar)

    z = ca_ref[...] * xa_ref[...]
    z1, z2 = _shifted(zpad, z, 2, ts)
    cwa = cwa_ref[...]
    y_a = ba_ref[...] * (z * cwa[2:3] + z1 * cwa[1:2] + z2 * cwa[0:1])
    zpad[0:SUBLANES, :] = z[ts - SUBLANES:, :]

    xb = xb_ref[...]
    x1, x2, x3 = _shifted(xpad, xb, 3, ts)
    cwb = cwb_ref[...]
    u = xb * cwb[3:4] + x1 * cwb[2:3] + x2 * cwb[1:2] + x3 * cwb[0:1] + cbb_ref[...]
    xpad[0:SUBLANES, :] = xb[ts - SUBLANES:, :]

    ub = u.astype(BF16)
    pre_a, pre_i = [], []
    for q in range(D // LANES):
        res = _dot(ub[:, q * LANES:(q + 1) * LANES], wg_ref[q])
        pre_a.append(res[:, :LANES])
        pre_i.append(res[:, LANES:])
    gate_a = _sigmoid(jnp.concatenate(pre_a, axis=-1) + bia_ref[...])
    gate_i = _sigmoid(jnp.concatenate(pre_i, axis=-1) + bii_ref[...])
    log_a = (-LRU_C) * gate_a * _softplus(ap_ref[...])
    a = jnp.exp(log_a)
    mult = jnp.sqrt(-jnp.tanh(log_a) * (1.0 + a * a))
    row = lax.broadcasted_iota(jnp.int32, (ts, D), 0)
    mult = jnp.where((row == 0) & (t == 0), 1.0, mult)
    h = u * gate_i * mult

    d = 1
    while d < ts:
        keep = row >= d
        h_sh = jnp.where(keep, pltpu.roll(h, d, 0), 0.0)
        a_sh = jnp.where(keep, pltpu.roll(a, d, 0), 1.0)
        h = h + a * h_sh
        a = a * a_sh
        d *= 2
    h = h + a * hcar[...]
    hcar[...] = h[ts - 1:ts, :]

    gb = gb_ref[...]
    gelu = 0.5 * gb * (1.0 + jnp.tanh(0.7978845608028654 * (gb + 0.044715 * gb * gb * gb)))
    y_b = h * gelu
    o_ref[...] = (_sigmoid(ga_ref[...] + mba_ref[...]) * y_a
                  + _sigmoid(gbb_ref[...] + mbb_ref[...]) * y_b)


def _mix_ab(p3, cwa, cwb, cbb, wg, bia, bii, ap, mba, mbb, ts):
    B, S, _ = p3.shape
    D = cwa.shape[-1]

    def col(c):
        return pl.BlockSpec((None, ts, D), lambda b, t, c=c: (b, t, c))

    def par(shape):
        nd = len(shape)
        return pl.BlockSpec(shape, lambda b, t, nd=nd: (0,) * nd)

    return pl.pallas_call(
        functools.partial(_mix_ab_kernel, ts=ts),
        grid=(B, S // ts),
        in_specs=[col(0), col(1), col(2), col(3), col(4), col(5), col(6),
                  par(cwa.shape), par(cwb.shape), par(cbb.shape), par(wg.shape),
                  par(bia.shape), par(bii.shape), par(ap.shape), par(mba.shape), par(mbb.shape)],
        out_specs=pl.BlockSpec((None, ts, D), lambda b, t: (b, t, 0)),
        out_shape=jax.ShapeDtypeStruct((B, S, D), F32),
        scratch_shapes=[pltpu.VMEM((ts + SUBLANES, D), F32), pltpu.VMEM((ts + SUBLANES, D), F32),
                        pltpu.VMEM((1, D), F32)],
        compiler_params=pltpu.CompilerParams(
            dimension_semantics=("parallel", "arbitrary"), vmem_limit_bytes=VMEM_LIMIT),
        name="mix_ab",
    )(p3, p3, p3, p3, p3, p3, p3, cwa, cwb, cbb, wg, bia, bii, ap, mba, mbb)


def _bmm(a, b):
    return jnp.einsum('nik,nkj->nij', a.astype(BF16), b.astype(BF16), preferred_element_type=F32)


def _bmm_nt(a, b):
    return jnp.einsum('nik,njk->nij', a.astype(BF16), b.astype(BF16), preferred_element_type=F32)


def _bmm_tn(a, b):
    return jnp.einsum('nti,ntj->nij', a.astype(BF16), b.astype(BF16), preferred_element_type=F32)


def _bd(x, bdmask):
    xb = x.astype(BF16)
    return jnp.where(bdmask, jnp.concatenate([xb, xb], axis=1), jnp.zeros((), BF16))


def _tri_inv(A, row, hcol, bdmask):
    def blk(s):
        return (row // s) == (hcol // s)

    def pmm(x, y):
        return _bmm(x, _bd(y, bdmask))

    L = A.shape[1]
    eye = (row == hcol).astype(F32)
    Ad = jnp.where(blk(SUBLANES), A, 0.0)
    T = eye + Ad
    P = Ad
    s = 2
    while s < SUBLANES:
        P = pmm(P, P)
        T = T + pmm(T, P)
        s *= 2
    s = SUBLANES
    while s < L:
        Aoff = jnp.where(blk(2 * s) & jnp.logical_not(blk(s)), A, 0.0)
        T = T + pmm(pmm(T, Aoff), T)
        s *= 2
    return T


def _rwkv_kernel(*refs, ts, hg, layer0):
    it = iter(refs)
    pr_ref, pk_ref, pv_ref, pwa_ref, pxg_ref, pgc_ref, mab_ref = [next(it) for _ in range(7)]
    if not layer0:
        hv_ref, vf_ref = next(it), next(it)
    (mur_ref, muk_ref, muv_ref, muwa_ref, mug_ref, w0_ref, a0_ref, kk_ref, ka_ref, rk_ref,
     lng_ref, lnb_ref, mbc_ref) = [next(it) for _ in range(13)]
    if not layer0:
        v0_ref = next(it)
    w2_ref, a2_ref, g2_ref = next(it), next(it), next(it)
    if not layer0:
        v2_ref = next(it)
    o_ref = next(it)
    if layer0:
        vfo_ref = next(it)
    rpad, kpad, vpad, wapad, xgpad, st_scr = [next(it) for _ in range(6)]

    t = pl.program_id(2)
    cw = hg * HEAD
    nc = ts // CHUNK
    npair = hg // 2
    L = CHUNK

    @pl.when(t == 0)
    def _():
        for pad in (rpad, kpad, vpad, wapad, xgpad):
            pad[0:SUBLANES, :] = jnp.zeros((SUBLANES, pad.shape[-1]), F32)
        st_scr[...] = jnp.zeros_like(st_scr)

    def token_shift(pad, x_ref, mu_ref):
        x = x_ref[...]
        (prev,) = _shifted(pad, x, 1, ts)
        pad[0:SUBLANES, :] = x[ts - SUBLANES:, :]
        return x + (prev - x) * mu_ref[...]

    r = token_shift(rpad, pr_ref, mur_ref)
    k = token_shift(kpad, pk_ref, muk_ref)
    v = token_shift(vpad, pv_ref, muv_ref)
    xwa = token_shift(wapad, pwa_ref, muwa_ref)
    xg = token_shift(xgpad, pxg_ref, mug_ref)

    wl = w0_ref[...] + _dot(jnp.tanh(xwa).astype(BF16), w2_ref[...])
    ld = -EXP_M05 * _sigmoid(wl)
    a = _sigmoid(a0_ref[...] + _dot(xwa.astype(BF16), a2_ref[...]))
    g = _dot(_sigmoid(xg).astype(BF16), g2_ref[...])
    if layer0:
        vfo_ref[...] = v
    else:
        mix = _sigmoid(v0_ref[...] + _dot(hv_ref[...].astype(BF16), v2_ref[...]))
        v = v + (vf_ref[...] - v) * mix

    ri = lax.broadcasted_iota(jnp.int32, (ts, ts), 0)
    ci = lax.broadcasted_iota(jnp.int32, (ts, ts), 1)
    same = (ri // L) == (ci // L)
    tri = jnp.where(same & (ci <= ri), 1.0, 0.0).astype(BF16)
    hw = min(cw, MXU_DIM)
    hi = lax.broadcasted_iota(jnp.int32, (hw, hw), 0)
    hj = lax.broadcasted_iota(jnp.int32, (hw, hw), 1)
    ones_h = jnp.where((hi // HEAD) == (hj // HEAD), 1.0, 0.0).astype(BF16)

    c = _sum_left(tri, ld)
    c3 = c.reshape(nc, L, cw)
    cl = jnp.broadcast_to(c3[:, L - 1:L, :], (nc, L, cw)).reshape(ts, cw)
    e_c = jnp.exp(c)
    e_n = jnp.exp(-c)
    e_l = jnp.exp(cl - c)

    kk = k * kk_ref[...]
    kk = kk * lax.rsqrt(jnp.maximum(_sum_right(kk * kk, ones_h), 1e-24))
    k = k * (1.0 + (a - 1.0) * ka_ref[...])
    b = kk * a
    Rt_f = (r * e_c).astype(BF16)
    At_f = (-kk * jnp.exp(c - ld)).astype(BF16)
    Bt_f = (b * e_n).astype(BF16)
    Kt_f = (k * e_n).astype(BF16)
    Bh_f = (b * e_l).astype(BF16)
    Kh_f = (k * e_l).astype(BF16)
    v_b = v.astype(BF16)

    PW = 2 * HEAD
    prow = lax.broadcasted_iota(jnp.int32, (L, PW), 0)
    hcol = lax.broadcasted_iota(jnp.int32, (L, PW), 1) % HEAD
    low_incl = hcol <= prow
    low_strict = hcol < prow
    brow = lax.broadcasted_iota(jnp.int32, (2 * L, PW), 0)
    bcol = lax.broadcasted_iota(jnp.int32, (2 * L, PW), 1)
    bdmask = (brow // L) == (bcol // HEAD)
    eye2 = brow == bcol

    def pairs(x):
        per_pair = [x[:, p * PW:(p + 1) * PW].reshape(nc, 1, L, PW) for p in range(npair)]
        return jnp.concatenate(per_pair, axis=1).reshape(nc * npair, L, PW)

    Rt, At, Bt, Kt, Bh, Kh, vp = [pairs(x) for x in (Rt_f, At_f, Bt_f, Kt_f, Bh_f, Kh_f, v_b)]
    zero_p = jnp.zeros_like(vp)
    AA = _bmm_nt(jnp.concatenate([Rt, At], axis=1),
                 jnp.concatenate([_bd(Bt, bdmask), _bd(Kt, bdmask)], axis=1))
    Arb = jnp.where(low_incl, AA[:, :L, :PW], 0.0)
    Ark = jnp.where(low_incl, AA[:, :L, PW:], 0.0)
    Aab = jnp.where(low_strict, AA[:, L:, :PW], 0.0)
    Aak = jnp.where(low_strict, AA[:, L:, PW:], 0.0)
    T = _tri_inv(Aab, prow, hcol, bdmask)
    AakV = _bmm(Aak, _bd(vp, bdmask))
    WU = _bmm(T, jnp.concatenate([_bd(At, bdmask), _bd(AakV, bdmask)], axis=2))
    W, U0 = WU[:, :, :PW], WU[:, :, PW:]
    QY = _bmm(jnp.concatenate([Arb, Ark], axis=2),
              jnp.concatenate([jnp.concatenate([_bd(W, bdmask), _bd(U0, bdmask)], axis=2),
                               jnp.concatenate([jnp.zeros((nc * npair, 2 * L, PW), BF16), _bd(vp, bdmask)],
                                               axis=2)], axis=1))
    Q = Rt.astype(F32) + QY[:, :, :PW]
    Y0 = QY[:, :, PW:]
    MG = _bmm_tn(jnp.concatenate([Bh, Kh], axis=1),
                 jnp.concatenate([WU.astype(BF16), jnp.concatenate([zero_p, vp], axis=2)], axis=1))
    pl_row = pairs(e_c)[:, L - 1:L, :]
    M = jnp.where(bdmask, MG[:, :, :PW], 0.0) + jnp.where(eye2, pl_row, 0.0)
    G = jnp.where(bdmask, MG[:, :, PW:], 0.0)
    QM = jnp.concatenate([Q, M], axis=1).astype(BF16)

    H = st_scr[...]
    ys = []
    for ch in range(nc):
        sl = slice(ch * npair, (ch + 1) * npair)
        YH = _bmm(QM[sl], H)
        ys.append(YH[:, :L, :] + Y0[sl])
        H = YH[:, L:, :] + G[sl]
    st_scr[...] = H
    y = jnp.concatenate(
        [jnp.concatenate([ys[ch][p] for ch in range(nc)], axis=0) for p in range(npair)], axis=-1)
    inv_n = 1.0 / HEAD
    yc = y - _sum_right(y, ones_h) * inv_n
    var = _sum_right(yc * yc, ones_h) * inv_n
    yn = yc * lax.rsqrt(var + LNX_EPS)
    bonus = _sum_right(r * k * rk_ref[...], ones_h)
    y_c = (yn * lng_ref[...] + lnb_ref[...] + bonus * v) * g
    m = mab_ref[...] + _sigmoid(pgc_ref[...] + mbc_ref[...]) * y_c
    o_ref[...] = m.astype(o_ref.dtype)


def _rwkv(p3, hv3, vf3, mab3, mu, w0, a0, kkp, kap, rkp, lng, lnb, mbc, v0, w2p, a2p, g2, v2p,
          col_r, col_k, col_v, col_wa, col_xg, col_gc, ts, hg):
    B, S, _ = p3.shape
    D = w0.shape[-1]
    cw = hg * HEAD
    layer0 = vf3 is None

    def colblk(c0, w, grouped):
        return pl.BlockSpec((None, ts, w), lambda b, gi, t: (b, t, c0 // w + (gi if grouped else 0)))

    def rowpar(c0, w, grouped):
        return pl.BlockSpec((1, w), lambda b, gi, t: (0, c0 // w + (gi if grouped else 0)))

    def matpar(rows):
        return pl.BlockSpec((rows, cw), lambda b, gi, t: (0, gi))

    act = pl.BlockSpec((None, ts, cw), lambda b, gi, t: (b, t, gi))
    in_specs = [colblk(col_r, cw, True), colblk(col_k, cw, True), colblk(col_v, cw, True),
                colblk(col_wa, LANES, False), colblk(col_xg, LANES, False), colblk(col_gc, cw, True), act]
    args = [p3, p3, p3, p3, p3, p3, mab3]
    if not layer0:
        in_specs += [pl.BlockSpec((None, ts, LANES), lambda b, gi, t: (b, t, 0)), act]
        args += [hv3, vf3]
    in_specs += [rowpar(0, cw, True), rowpar(D, cw, True), rowpar(2 * D, cw, True),
                 rowpar(3 * D, LANES, False), rowpar(3 * D + LANES, LANES, False)]
    args += [mu] * 5
    in_specs += [rowpar(0, cw, True)] * 8
    args += [w0, a0, kkp, kap, rkp, lng, lnb, mbc]
    if not layer0:
        in_specs.append(rowpar(0, cw, True))
        args.append(v0)
    in_specs += [matpar(LANES), matpar(LANES), matpar(LANES)]
    args += [w2p, a2p, g2]
    if not layer0:
        in_specs.append(matpar(LANES))
        args.append(v2p)
    out_shape = [jax.ShapeDtypeStruct((B, S, D), BF16)]
    out_specs = [act]
    if layer0:
        out_shape.append(jax.ShapeDtypeStruct((B, S, D), F32))
        out_specs.append(act)
    res = pl.pallas_call(
        functools.partial(_rwkv_kernel, ts=ts, hg=hg, layer0=layer0),
        grid=(B, D // cw, S // ts),
        in_specs=in_specs, out_specs=out_specs, out_shape=out_shape,
        scratch_shapes=[pltpu.VMEM((ts + SUBLANES, cw), F32)] * 3
        + [pltpu.VMEM((ts + SUBLANES, LANES), F32)] * 2
        + [pltpu.VMEM((hg // 2, 2 * HEAD, 2 * HEAD), F32)],
        compiler_params=pltpu.CompilerParams(
            dimension_semantics=("parallel", "parallel", "arbitrary"), vmem_limit_bytes=VMEM_LIMIT),
        name="rwkv",
    )(*args)
    return (res[0], res[1]) if layer0 else (res[0], vf3)


def _mlp_kernel(x_ref, m_ref, wo_ref, g2_ref, w1_ref, w2_ref, fg_ref, o_ref, acc, h2_scr, *, final):
    j = pl.program_id(1)

    @pl.when(j == 0)
    def _():
        xn = x_ref[...] + _dot(m_ref[...], wo_ref[...])
        acc[...] = xn
        h2_scr[...] = _rms(xn, g2_ref[...]).astype(BF16)

    hid = jnp.square(jnp.maximum(_dot(h2_scr[...], w1_ref[...]), 0.0))
    acc[...] += _dot(hid.astype(BF16), w2_ref[...])

    @pl.when(j == pl.num_programs(1) - 1)
    def _():
        o_ref[...] = _rms(acc[...], fg_ref[...]) if final else acc[...]


def _mlp(x2, m2, wo_bf, g2, w1_bf, w2_bf, fg, tm, tf, final):
    T, D = x2.shape
    FF = w1_bf.shape[1]
    return pl.pallas_call(
        functools.partial(_mlp_kernel, final=final),
        grid=(T // tm, FF // tf),
        in_specs=[pl.BlockSpec((tm, D), lambda i, j: (i, 0)),
                  pl.BlockSpec((tm, D), lambda i, j: (i, 0)),
                  pl.BlockSpec((D, D), lambda i, j: (0, 0)),
                  pl.BlockSpec((1, D), lambda i, j: (0, 0)),
                  pl.BlockSpec((D, tf), lambda i, j: (0, j)),
                  pl.BlockSpec((tf, D), lambda i, j: (j, 0)),
                  pl.BlockSpec((1, D), lambda i, j: (0, 0))],
        out_specs=pl.BlockSpec((tm, D), lambda i, j: (i, 0)),
        out_shape=jax.ShapeDtypeStruct((T, D), F32),
        scratch_shapes=[pltpu.VMEM((tm, D), F32), pltpu.VMEM((tm, D), BF16)],
        compiler_params=pltpu.CompilerParams(
            dimension_semantics=("parallel", "arbitrary"), vmem_limit_bytes=VMEM_LIMIT),
        name="mlp",
    )(x2, m2, wo_bf, g2, w1_bf, w2_bf, fg)


def _tile(n, pref):
    t = min(n, pref)
    while n % t:
        t //= 2
    return t


def _col_tile(n, pref):
    best = LANES
    for m in range(1, n // LANES + 1):
        if (n // LANES) % m == 0 and m * LANES <= pref:
            best = m * LANES
    return best


def kernel(x, norm1_g, w_in, merge_b, conv_a_w, lru_conv_w, lru_conv_b, lru_wa, lru_ba, lru_wi, lru_bi,
           lru_a_param, rwkv_mu, rwkv_w0, rwkv_w2, rwkv_a0, rwkv_a2, rwkv_g2, rwkv_kk, rwkv_ka, rwkv_rk,
           rwkv_lnx_g, rwkv_lnx_b, rwkv_v0, rwkv_v1, rwkv_v2, w_out, norm2_g, mlp_w1, mlp_w2, final_g):
    B, S, D = x.shape
    depth = w_in.shape[0]
    T = B * S
    n_in = w_in.shape[-1]
    col_gate = 3 * D + 2 * D
    col_c = col_gate + 3 * D
    col_wa = col_c + 3 * D
    col_xg = col_wa + R_W + R_A
    assert n_in == col_xg + R_G and R_W + R_A == LANES and R_G == LANES and D % LANES == 0

    tm = _tile(T, 1024)
    tn = _col_tile(n_in, 1280)
    ts_ab = _tile(S, 256)
    ts_c = _tile(S, 256)
    hg = 8
    tm_mlp = _tile(T, 512)
    tf = _tile(mlp_w1.shape[-1], 1024)

    row = lambda vec: vec.reshape(1, -1)
    x2 = x.reshape(T, D)
    vf3 = None
    for l in range(depth):
        w_bf = w_in[l].astype(BF16)
        v1p = v2p = None
        if l > 0:
            v1p = jnp.pad(rwkv_v1[l - 1], ((0, 0), (0, LANES - R_V))).astype(BF16)
            v2p = jnp.pad(rwkv_v2[l - 1], ((0, LANES - R_V), (0, 0))).astype(BF16)
        w2p = jnp.pad(rwkv_w2[l], ((0, R_A), (0, 0))).astype(BF16)
        a2p = jnp.pad(rwkv_a2[l], ((R_W, 0), (0, 0))).astype(BF16)
        g2 = rwkv_g2[l].astype(BF16)
        bw = D // LRU_HEADS
        wa = lru_wa[l].reshape(LRU_HEADS // 2, 2, bw, bw)
        wi = lru_wi[l].reshape(LRU_HEADS // 2, 2, bw, bw)
        z = jnp.zeros_like(wa[:, 0])
        blockdiag = lambda w: jnp.concatenate(
            [jnp.concatenate([w[:, 0], z], axis=-1), jnp.concatenate([z, w[:, 1]], axis=-1)], axis=-2)
        wg = jnp.concatenate([blockdiag(wa), blockdiag(wi)], axis=-1).astype(BF16)
        mb = merge_b[l]

        p2, hv2 = _inproj(x2, row(norm1_g[l]), w_bf, v1p, tm, tn)
        p3 = p2.reshape(B, S, n_in)
        mab3 = _mix_ab(p3, conv_a_w[l], lru_conv_w[l], row(lru_conv_b[l]), wg, row(lru_ba[l]), row(lru_bi[l]),
                       row(lru_a_param[l]), row(mb[:D]), row(mb[D:2 * D]), ts_ab)
        m3, vf3 = _rwkv(p3, None if l == 0 else hv2.reshape(B, S, LANES), vf3, mab3,
                        row(rwkv_mu[l]), row(rwkv_w0[l]), row(rwkv_a0[l]), row(rwkv_kk[l]), row(rwkv_ka[l]),
                        row(rwkv_rk[l]), row(rwkv_lnx_g[l]), row(rwkv_lnx_b[l]), row(mb[2 * D:]),
                        None if l == 0 else row(rwkv_v0[l - 1]), w2p, a2p, g2, v2p,
                        col_c, col_c + D, col_c + 2 * D, col_wa, col_xg, col_gate + 2 * D, ts_c, hg)
        x2 = _mlp(x2, m3.reshape(T, D), w_out[l].astype(BF16), row(norm2_g[l]), mlp_w1[l].astype(BF16),
                  mlp_w2[l].astype(BF16), row(final_g), tm_mlp, tf, final=(l == depth - 1))
    return x2.reshape(B, S, D)
```

```python
import functools

import jax
import jax.numpy as jnp
from jax import lax
from jax.experimental import pallas as pl
from jax.experimental.pallas import tpu as pltpu

F32 = jnp.float32
BF16 = jnp.bfloat16

EPS = 1e-6
LRU_C = 8.0
LRU_HEADS = 16
HEAD = 64
CHUNK = 64
LNX_EPS = HEAD * 1e-5
EXP_M05 = 0.6065306597126334
R_W, R_A, R_G, R_V = 64, 64, 128, 32
LANES = 128
SUBLANES = 8
MXU_DIM = 256
VMEM_LIMIT = 48 * 1024 * 1024


def _dot(a, b):
    return jnp.dot(a, b, preferred_element_type=F32)


def _sigmoid(x):
    return 1.0 / (1.0 + jnp.exp(-x))


def _softplus(x):
    return jnp.maximum(x, 0.0) + jnp.log1p(jnp.exp(-jnp.abs(x)))


def _rms(x, g):
    return x * lax.rsqrt(jnp.mean(x * x, axis=-1, keepdims=True) + EPS) * g


def _head_sums(x):
    lo = lax.broadcasted_iota(jnp.int32, (x.shape[0], LANES), 1) < HEAD
    cols = []
    for i in range(0, x.shape[1], LANES):
        xi = x[:, i:i + LANES]
        s_lo = jnp.sum(jnp.where(lo, xi, 0.0), axis=-1, keepdims=True)
        s_hi = jnp.sum(jnp.where(lo, 0.0, xi), axis=-1, keepdims=True)
        cols.append(jnp.where(lo, s_lo, s_hi))
    return cols[0] if len(cols) == 1 else jnp.concatenate(cols, axis=1)


def _sum_left(ones_bf, x):
    p1 = x.astype(BF16)
    p2 = (x - p1.astype(F32)).astype(BF16)
    return _dot(ones_bf, p1) + _dot(ones_bf, p2)


def _inproj_kernel(*refs, has_v1):
    if has_v1:
        x_ref, g_ref, w_ref, v1_ref, p_ref, hv_ref, h_scr = refs
    else:
        x_ref, g_ref, w_ref, p_ref, h_scr = refs

    @pl.when(pl.program_id(1) == 0)
    def _():
        hb = _rms(x_ref[...], g_ref[...]).astype(BF16)
        h_scr[...] = hb
        if has_v1:
            hv_ref[...] = _dot(hb, v1_ref[...])

    p_ref[...] = _dot(h_scr[...], w_ref[...])


def _inproj(x2, g, w_bf, v1p_bf, tm, tn):
    T, D = x2.shape
    N = w_bf.shape[1]
    has_v1 = v1p_bf is not None
    in_specs = [pl.BlockSpec((tm, D), lambda i, j: (i, 0)),
                pl.BlockSpec((1, D), lambda i, j: (0, 0)),
                pl.BlockSpec((D, tn), lambda i, j: (0, j))]
    args = [x2, g, w_bf]
    out_shape = [jax.ShapeDtypeStruct((T, N), F32)]
    out_specs = [pl.BlockSpec((tm, tn), lambda i, j: (i, j))]
    if has_v1:
        in_specs.append(pl.BlockSpec((D, LANES), lambda i, j: (0, 0)))
        args.append(v1p_bf)
        out_shape.append(jax.ShapeDtypeStruct((T, LANES), F32))
        out_specs.append(pl.BlockSpec((tm, LANES), lambda i, j: (i, 0)))
    res = pl.pallas_call(
        functools.partial(_inproj_kernel, has_v1=has_v1),
        grid=(T // tm, N // tn),
        in_specs=in_specs, out_specs=out_specs, out_shape=out_shape,
        scratch_shapes=[pltpu.VMEM((tm, D), BF16)],
        compiler_params=pltpu.CompilerParams(
            dimension_semantics=("parallel", "arbitrary"), vmem_limit_bytes=VMEM_LIMIT),
        name="inproj",
    )(*args)
    return (res[0], res[1]) if has_v1 else (res[0], None)


def _shifted(tail_ref, x, j, ts):
    tail = tail_ref[...]
    sub = lax.broadcasted_iota(jnp.int32, tail.shape, 0)
    out = []
    for d in range(1, j + 1):
        rolled = pltpu.roll(x, d, 0)
        first = jnp.where(sub < d, pltpu.roll(tail, d, 0), rolled[:SUBLANES])
        out.append(jnp.concatenate([first, rolled[SUBLANES:]], axis=0))
    tail_ref[...] = x[ts - SUBLANES:, :]
    return out


def _mix_ab_kernel(ba_ref, ca_ref, xa_ref, xb_ref, gb_ref, ga_ref, gbb_ref,
                   cwa_ref, cwb_ref, cbb_ref, wg_ref, bia_ref, bii_ref, ap_ref, mba_ref, mbb_ref,
                   o_ref, zpad, xpad, hcar, *, ts):
    t = pl.program_id(1)
    D = o_ref.shape[-1]

    @pl.when(t == 0)
    def _():
        zpad[...] = jnp.zeros_like(zpad)
        xpad[...] = jnp.zeros_like(xpad)
        hcar[...] = jnp.zeros_like(hcar)

    z = ca_ref[...] * xa_ref[...]
    z1, z2 = _shifted(zpad, z, 2, ts)
    cwa = cwa_ref[...]
    y_a = ba_ref[...] * (z * cwa[2:3] + z1 * cwa[1:2] + z2 * cwa[0:1])

    xb = xb_ref[...]
    x1, x2, x3 = _shifted(xpad, xb, 3, ts)
    cwb = cwb_ref[...]
    u = xb * cwb[3:4] + x1 * cwb[2:3] + x2 * cwb[1:2] + x3 * cwb[0:1] + cbb_ref[...]

    ub = u.astype(BF16)
    pre_a, pre_i = [], []
    for q in range(D // LANES):
        res = _dot(ub[:, q * LANES:(q + 1) * LANES], wg_ref[q])
        pre_a.append(res[:, :LANES])
        pre_i.append(res[:, LANES:])
    gate_a = _sigmoid(jnp.concatenate(pre_a, axis=-1) + bia_ref[...])
    gate_i = _sigmoid(jnp.concatenate(pre_i, axis=-1) + bii_ref[...])
    log_a = (-LRU_C) * gate_a * _softplus(ap_ref[...])
    a = jnp.exp(log_a)
    mult = jnp.sqrt(-jnp.tanh(log_a) * (1.0 + a * a))
    row = lax.broadcasted_iota(jnp.int32, (ts, D), 0)
    mult = jnp.where((row == 0) & (t == 0), 1.0, mult)
    h = u * gate_i * mult

    sub = row % SUBLANES
    d = 1
    while d < SUBLANES:
        keep = sub >= d
        h_sh = jnp.where(keep, pltpu.roll(h, d, 0), 0.0)
        a_sh = jnp.where(keep, pltpu.roll(a, d, 0), 1.0)
        h = h + a * h_sh
        a = a * a_sh
        d *= 2
    carry = hcar[...]
    groups = []
    for gi in range(ts // SUBLANES):
        rows = slice(gi * SUBLANES, (gi + 1) * SUBLANES)
        hgrp = h[rows] + a[rows] * carry
        groups.append(hgrp)
        carry = hgrp[SUBLANES - 1:SUBLANES]
    h = jnp.concatenate(groups, axis=0)
    hcar[...] = carry

    gb = gb_ref[...]
    gelu = 0.5 * gb * (1.0 + jnp.tanh(0.7978845608028654 * (gb + 0.044715 * gb * gb * gb)))
    y_b = h * gelu
    o_ref[...] = (_sigmoid(ga_ref[...] + mba_ref[...]) * y_a
                  + _sigmoid(gbb_ref[...] + mbb_ref[...]) * y_b)


def _mix_ab(p3, cwa, cwb, cbb, wg, bia, bii, ap, mba, mbb, ts):
    B, S, _ = p3.shape
    D = cwa.shape[-1]

    def col(c):
        return pl.BlockSpec((None, ts, D), lambda b, t, c=c: (b, t, c))

    def par(shape):
        nd = len(shape)
        return pl.BlockSpec(shape, lambda b, t, nd=nd: (0,) * nd)

    return pl.pallas_call(
        functools.partial(_mix_ab_kernel, ts=ts),
        grid=(B, S // ts),
        in_specs=[col(0), col(1), col(2), col(3), col(4), col(5), col(6),
                  par(cwa.shape), par(cwb.shape), par(cbb.shape), par(wg.shape),
                  par(bia.shape), par(bii.shape), par(ap.shape), par(mba.shape), par(mbb.shape)],
        out_specs=pl.BlockSpec((None, ts, D), lambda b, t: (b, t, 0)),
        out_shape=jax.ShapeDtypeStruct((B, S, D), F32),
        scratch_shapes=[pltpu.VMEM((SUBLANES, D), F32), pltpu.VMEM((SUBLANES, D), F32),
                        pltpu.VMEM((1, D), F32)],
        compiler_params=pltpu.CompilerParams(
            dimension_semantics=("parallel", "arbitrary"), vmem_limit_bytes=VMEM_LIMIT),
        name="mix_ab",
    )(p3, p3, p3, p3, p3, p3, p3, cwa, cwb, cbb, wg, bia, bii, ap, mba, mbb)


def _bmm(a, b):
    return jnp.einsum('nik,nkj->nij', a.astype(BF16), b.astype(BF16), preferred_element_type=F32)


def _bmm_nt(a, b):
    return jnp.einsum('nik,njk->nij', a.astype(BF16), b.astype(BF16), preferred_element_type=F32)


def _bmm_tn(a, b):
    return jnp.einsum('nti,ntj->nij', a.astype(BF16), b.astype(BF16), preferred_element_type=F32)


def _bd(x, bdmask):
    xb = x.astype(BF16)
    return jnp.where(bdmask, jnp.concatenate([xb, xb], axis=1), jnp.zeros((), BF16))


def _tri_inv(A, row, hcol, bdmask):
    def blk(s):
        return (row // s) == (hcol // s)

    def pmm(x, y):
        return _bmm(x, _bd(y, bdmask))

    L = A.shape[1]
    eye = (row == hcol).astype(F32)
    Ad = jnp.where(blk(SUBLANES), A, 0.0)
    T = eye + Ad
    P = Ad
    s = 2
    while s < SUBLANES:
        P = pmm(P, P)
        T = T + pmm(T, P)
        s *= 2
    s = SUBLANES
    while s < L:
        Aoff = jnp.where(blk(2 * s) & jnp.logical_not(blk(s)), A, 0.0)
        T = T + pmm(pmm(T, Aoff), T)
        s *= 2
    return T


def _rwkv_kernel(*refs, ts, hg, layer0):
    it = iter(refs)
    pr_ref, pk_ref, pv_ref, pwa_ref, pxg_ref, pgc_ref, mab_ref = [next(it) for _ in range(7)]
    if not layer0:
        hv_ref, vf_ref = next(it), next(it)
    (mur_ref, muk_ref, muv_ref, muwa_ref, mug_ref, w0_ref, a0_ref, kk_ref, ka_ref, rk_ref,
     lng_ref, lnb_ref, mbc_ref) = [next(it) for _ in range(13)]
    if not layer0:
        v0_ref = next(it)
    w2_ref, a2_ref, g2_ref = next(it), next(it), next(it)
    if not layer0:
        v2_ref = next(it)
    o_ref = next(it)
    if layer0:
        vfo_ref = next(it)
    rpad, kpad, vpad, wapad, xgpad, st_scr = [next(it) for _ in range(6)]

    t = pl.program_id(2)
    cw = hg * HEAD
    nc = ts // CHUNK
    npair = hg // 2
    L = CHUNK

    @pl.when(t == 0)
    def _():
        for pad in (rpad, kpad, vpad, wapad, xgpad):
            pad[...] = jnp.zeros_like(pad)
        st_scr[...] = jnp.zeros_like(st_scr)

    def token_shift(pad, x_ref, mu_ref):
        x = x_ref[...]
        (prev,) = _shifted(pad, x, 1, ts)
        return x + (prev - x) * mu_ref[...]

    r = token_shift(rpad, pr_ref, mur_ref)
    k = token_shift(kpad, pk_ref, muk_ref)
    v = token_shift(vpad, pv_ref, muv_ref)
    xwa = token_shift(wapad, pwa_ref, muwa_ref)
    xg = token_shift(xgpad, pxg_ref, mug_ref)

    wl = w0_ref[...] + _dot(jnp.tanh(xwa).astype(BF16), w2_ref[...])
    ld = -EXP_M05 * _sigmoid(wl)
    a = _sigmoid(a0_ref[...] + _dot(xwa.astype(BF16), a2_ref[...]))
    g = _dot(_sigmoid(xg).astype(BF16), g2_ref[...])
    if layer0:
        vfo_ref[...] = v
    else:
        mix = _sigmoid(v0_ref[...] + _dot(hv_ref[...].astype(BF16), v2_ref[...]))
        v = v + (vf_ref[...] - v) * mix

    ri = lax.broadcasted_iota(jnp.int32, (ts, ts), 0)
    ci = lax.broadcasted_iota(jnp.int32, (ts, ts), 1)
    tri = jnp.where(((ri // L) == (ci // L)) & (ci <= ri), 1.0, 0.0).astype(BF16)

    c = _sum_left(tri, ld)
    c3 = c.reshape(nc, L, cw)
    cl = jnp.broadcast_to(c3[:, L - 1:L, :], (nc, L, cw)).reshape(ts, cw)
    e_c = jnp.exp(c)
    e_n = jnp.exp(-c)
    e_l = jnp.exp(cl - c)

    kk = k * kk_ref[...]
    kk = kk * lax.rsqrt(jnp.maximum(_head_sums(kk * kk), 1e-24))
    k = k * (1.0 + (a - 1.0) * ka_ref[...])
    b = kk * a
    Rt_f = (r * e_c).astype(BF16)
    At_f = (-kk * jnp.exp(c - ld)).astype(BF16)
    Bt_f = (b * e_n).astype(BF16)
    Kt_f = (k * e_n).astype(BF16)
    Bh_f = (b * e_l).astype(BF16)
    Kh_f = (k * e_l).astype(BF16)
    v_b = v.astype(BF16)

    PW = 2 * HEAD
    prow = lax.broadcasted_iota(jnp.int32, (L, PW), 0)
    hcol = lax.broadcasted_iota(jnp.int32, (L, PW), 1) % HEAD
    low_incl = hcol <= prow
    low_strict = hcol < prow
    brow = lax.broadcasted_iota(jnp.int32, (2 * L, PW), 0)
    bcol = lax.broadcasted_iota(jnp.int32, (2 * L, PW), 1)
    bdmask = (brow // L) == (bcol // HEAD)
    eye2 = brow == bcol

    def pairs(x):
        per_pair = [x[:, p * PW:(p + 1) * PW].reshape(nc, 1, L, PW) for p in range(npair)]
        return jnp.concatenate(per_pair, axis=1).reshape(nc * npair, L, PW)

    Rt, At, Bt, Kt, Bh, Kh, vp = [pairs(x) for x in (Rt_f, At_f, Bt_f, Kt_f, Bh_f, Kh_f, v_b)]
    zero_p = jnp.zeros_like(vp)
    AA = _bmm_nt(jnp.concatenate([Rt, At], axis=1),
                 jnp.concatenate([_bd(Bt, bdmask), _bd(Kt, bdmask)], axis=1))
    Arb = jnp.where(low_incl, AA[:, :L, :PW], 0.0)
    Ark = jnp.where(low_incl, AA[:, :L, PW:], 0.0)
    Aab = jnp.where(low_strict, AA[:, L:, :PW], 0.0)
    Aak = jnp.where(low_strict, AA[:, L:, PW:], 0.0)
    T = _tri_inv(Aab, prow, hcol, bdmask)
    AakV = _bmm(Aak, _bd(vp, bdmask))
    WU = _bmm(T, jnp.concatenate([_bd(At, bdmask), _bd(AakV, bdmask)], axis=2))
    W, U0 = WU[:, :, :PW], WU[:, :, PW:]
    QY = _bmm(jnp.concatenate([Arb, Ark], axis=2),
              jnp.concatenate([jnp.concatenate([_bd(W, bdmask), _bd(U0, bdmask)], axis=2),
                               jnp.concatenate([jnp.zeros((nc * npair, 2 * L, PW), BF16), _bd(vp, bdmask)],
                                               axis=2)], axis=1))
    Q = Rt.astype(F32) + QY[:, :, :PW]
    Y0 = QY[:, :, PW:]
    MG = _bmm_tn(jnp.concatenate([Bh, Kh], axis=1),
                 jnp.concatenate([WU.astype(BF16), jnp.concatenate([zero_p, vp], axis=2)], axis=1))
    pl_row = pairs(e_c)[:, L - 1:L, :]
    M = jnp.where(bdmask, MG[:, :, :PW], 0.0) + jnp.where(eye2, pl_row, 0.0)
    G = jnp.where(bdmask, MG[:, :, PW:], 0.0)
    QM = jnp.concatenate([Q, M], axis=1).astype(BF16)

    H = st_scr[...]
    ys = []
    for ch in range(nc):
        sl = slice(ch * npair, (ch + 1) * npair)
        YH = _bmm(QM[sl], H)
        ys.append(YH[:, :L, :] + Y0[sl])
        H = YH[:, L:, :] + G[sl]
    st_scr[...] = H
    y = jnp.concatenate(
        [jnp.concatenate([ys[ch][p] for ch in range(nc)], axis=0) for p in range(npair)], axis=-1)
    inv_n = 1.0 / HEAD
    yc = y - _head_sums(y) * inv_n
    var = _head_sums(yc * yc) * inv_n
    yn = yc * lax.rsqrt(var + LNX_EPS)
    bonus = _head_sums(r * k * rk_ref[...])
    y_c = (yn * lng_ref[...] + lnb_ref[...] + bonus * v) * g
    m = mab_ref[...] + _sigmoid(pgc_ref[...] + mbc_ref[...]) * y_c
    o_ref[...] = m.astype(o_ref.dtype)


def _rwkv(p3, hv3, vf3, mab3, mu, w0, a0, kkp, kap, rkp, lng, lnb, mbc, v0, w2p, a2p, g2, v2p,
          col_r, col_k, col_v, col_wa, col_xg, col_gc, ts, hg):
    B, S, _ = p3.shape
    D = w0.shape[-1]
    cw = hg * HEAD
    layer0 = vf3 is None

    def colblk(c0, w, grouped):
        return pl.BlockSpec((None, ts, w), lambda b, gi, t: (b, t, c0 // w + (gi if grouped else 0)))

    def rowpar(c0, w, grouped):
        return pl.BlockSpec((1, w), lambda b, gi, t: (0, c0 // w + (gi if grouped else 0)))

    def matpar(rows):
        return pl.BlockSpec((rows, cw), lambda b, gi, t: (0, gi))

    act = pl.BlockSpec((None, ts, cw), lambda b, gi, t: (b, t, gi))
    in_specs = [colblk(col_r, cw, True), colblk(col_k, cw, True), colblk(col_v, cw, True),
                colblk(col_wa, LANES, False), colblk(col_xg, LANES, False), colblk(col_gc, cw, True), act]
    args = [p3, p3, p3, p3, p3, p3, mab3]
    if not layer0:
        in_specs += [pl.BlockSpec((None, ts, LANES), lambda b, gi, t: (b, t, 0)), act]
        args += [hv3, vf3]
    in_specs += [rowpar(0, cw, True), rowpar(D, cw, True), rowpar(2 * D, cw, True),
                 rowpar(3 * D, LANES, False), rowpar(3 * D + LANES, LANES, False)]
    args += [mu] * 5
    in_specs += [rowpar(0, cw, True)] * 8
    args += [w0, a0, kkp, kap, rkp, lng, lnb, mbc]
    if not layer0:
        in_specs.append(rowpar(0, cw, True))
        args.append(v0)
    in_specs += [matpar(LANES), matpar(LANES), matpar(LANES)]
    args += [w2p, a2p, g2]
    if not layer0:
        in_specs.append(matpar(LANES))
        args.append(v2p)
    out_shape = [jax.ShapeDtypeStruct((B, S, D), BF16)]
    out_specs = [act]
    if layer0:
        out_shape.append(jax.ShapeDtypeStruct((B, S, D), F32))
        out_specs.append(act)
    res = pl.pallas_call(
        functools.partial(_rwkv_kernel, ts=ts, hg=hg, layer0=layer0),
        grid=(B, D // cw, S // ts),
        in_specs=in_specs, out_specs=out_specs, out_shape=out_shape,
        scratch_shapes=[pltpu.VMEM((SUBLANES, cw), F32)] * 3
        + [pltpu.VMEM((SUBLANES, LANES), F32)] * 2
        + [pltpu.VMEM((hg // 2, 2 * HEAD, 2 * HEAD), F32)],
        compiler_params=pltpu.CompilerParams(
            dimension_semantics=("parallel", "parallel", "arbitrary"), vmem_limit_bytes=VMEM_LIMIT),
        name="rwkv",
    )(*args)
    return (res[0], res[1]) if layer0 else (res[0], vf3)


def _mlp_kernel(x_ref, m_ref, wo_ref, g2_ref, w1_ref, w2_ref, fg_ref, o_ref, acc, h2_scr, *, final):
    j = pl.program_id(1)

    @pl.when(j == 0)
    def _():
        xn = x_ref[...] + _dot(m_ref[...], wo_ref[...])
        acc[...] = xn
        h2_scr[...] = _rms(xn, g2_ref[...]).astype(BF16)

    hid = jnp.square(jnp.maximum(_dot(h2_scr[...], w1_ref[...]), 0.0))
    acc[...] += _dot(hid.astype(BF16), w2_ref[...])

    @pl.when(j == pl.num_programs(1) - 1)
    def _():
        o_ref[...] = _rms(acc[...], fg_ref[...]) if final else acc[...]


def _mlp(x2, m2, wo_bf, g2, w1_bf, w2_bf, fg, tm, tf, final):
    T, D = x2.shape
    FF = w1_bf.shape[1]
    return pl.pallas_call(
        functools.partial(_mlp_kernel, final=final),
        grid=(T // tm, FF // tf),
        in_specs=[pl.BlockSpec((tm, D), lambda i, j: (i, 0)),
                  pl.BlockSpec((tm, D), lambda i, j: (i, 0)),
                  pl.BlockSpec((D, D), lambda i, j: (0, 0)),
                  pl.BlockSpec((1, D), lambda i, j: (0, 0)),
                  pl.BlockSpec((D, tf), lambda i, j: (0, j)),
                  pl.BlockSpec((tf, D), lambda i, j: (j, 0)),
                  pl.BlockSpec((1, D), lambda i, j: (0, 0))],
        out_specs=pl.BlockSpec((tm, D), lambda i, j: (i, 0)),
        out_shape=jax.ShapeDtypeStruct((T, D), F32),
        scratch_shapes=[pltpu.VMEM((tm, D), F32), pltpu.VMEM((tm, D), BF16)],
        compiler_params=pltpu.CompilerParams(
            dimension_semantics=("parallel", "arbitrary"), vmem_limit_bytes=VMEM_LIMIT),
        name="mlp",
    )(x2, m2, wo_bf, g2, w1_bf, w2_bf, fg)


def _tile(n, pref):
    t = min(n, pref)
    while n % t:
        t //= 2
    return t


def _col_tile(n, pref):
    best = LANES
    for m in range(1, n // LANES + 1):
        if (n // LANES) % m == 0 and m * LANES <= pref:
            best = m * LANES
    return best


def kernel(x, norm1_g, w_in, merge_b, conv_a_w, lru_conv_w, lru_conv_b, lru_wa, lru_ba, lru_wi, lru_bi,
           lru_a_param, rwkv_mu, rwkv_w0, rwkv_w2, rwkv_a0, rwkv_a2, rwkv_g2, rwkv_kk, rwkv_ka, rwkv_rk,
           rwkv_lnx_g, rwkv_lnx_b, rwkv_v0, rwkv_v1, rwkv_v2, w_out, norm2_g, mlp_w1, mlp_w2, final_g):
    B, S, D = x.shape
    depth = w_in.shape[0]
    T = B * S
    n_in = w_in.shape[-1]
    col_gate = 3 * D + 2 * D
    col_c = col_gate + 3 * D
    col_wa = col_c + 3 * D
    col_xg = col_wa + R_W + R_A
    assert n_in == col_xg + R_G and R_W + R_A == LANES and R_G == LANES and D % LANES == 0

    tm = _tile(T, 2048)
    tn = _col_tile(n_in, 768)
    ts_ab = _tile(S, 256)
    ts_c = _tile(S, 256)
    hg = 16
    tm_mlp = _tile(T, 512)
    tf = _tile(mlp_w1.shape[-1], 1024)

    row = lambda vec: vec.reshape(1, -1)
    x2 = x.reshape(T, D)
    vf3 = None
    for l in range(depth):
        w_bf = w_in[l].astype(BF16)
        v1p = v2p = None
        if l > 0:
            v1p = jnp.pad(rwkv_v1[l - 1], ((0, 0), (0, LANES - R_V))).astype(BF16)
            v2p = jnp.pad(rwkv_v2[l - 1], ((0, LANES - R_V), (0, 0))).astype(BF16)
        w2p = jnp.pad(rwkv_w2[l], ((0, R_A), (0, 0))).astype(BF16)
        a2p = jnp.pad(rwkv_a2[l], ((R_W, 0), (0, 0))).astype(BF16)
        g2 = rwkv_g2[l].astype(BF16)
        bw = D // LRU_HEADS
        wa = lru_wa[l].reshape(LRU_HEADS // 2, 2, bw, bw)
        wi = lru_wi[l].reshape(LRU_HEADS // 2, 2, bw, bw)
        z = jnp.zeros_like(wa[:, 0])
        blockdiag = lambda w: jnp.concatenate(
            [jnp.concatenate([w[:, 0], z], axis=-1), jnp.concatenate([z, w[:, 1]], axis=-1)], axis=-2)
        wg = jnp.concatenate([blockdiag(wa), blockdiag(wi)], axis=-1).astype(BF16)
        mb = merge_b[l]

        p2, hv2 = _inproj(x2, row(norm1_g[l]), w_bf, v1p, tm, tn)
        p3 = p2.reshape(B, S, n_in)
        mab3 = _mix_ab(p3, conv_a_w[l], lru_conv_w[l], row(lru_conv_b[l]), wg, row(lru_ba[l]), row(lru_bi[l]),
                       row(lru_a_param[l]), row(mb[:D]), row(mb[D:2 * D]), ts_ab)
        m3, vf3 = _rwkv(p3, None if l == 0 else hv2.reshape(B, S, LANES), vf3, mab3,
                        row(rwkv_mu[l]), row(rwkv_w0[l]), row(rwkv_a0[l]), row(rwkv_kk[l]), row(rwkv_ka[l]),
                        row(rwkv_rk[l]), row(rwkv_lnx_g[l]), row(rwkv_lnx_b[l]), row(mb[2 * D:]),
                        None if l == 0 else row(rwkv_v0[l - 1]), w2p, a2p, g2, v2p,
                        col_c, col_c + D, col_c + 2 * D, col_wa, col_xg, col_gate + 2 * D, ts_c, hg)
        x2 = _mlp(x2, m3.reshape(T, D), w_out[l].astype(BF16), row(norm2_g[l]), mlp_w1[l].astype(BF16),
                  mlp_w2[l].astype(BF16), row(final_g), tm_mlp, tf, final=(l == depth - 1))
    return x2.reshape(B, S, D)
```

```python
import functools

import jax
import jax.numpy as jnp
from jax import lax
from jax.experimental import pallas as pl
from jax.experimental.pallas import tpu as pltpu

F32 = jnp.float32
BF16 = jnp.bfloat16

EPS = 1e-6
LRU_C = 8.0
LRU_HEADS = 16
HEAD = 64
CHUNK = 64
LNX_EPS = HEAD * 1e-5
EXP_M05 = 0.6065306597126334
R_W, R_A, R_G, R_V = 64, 64, 128, 32
LANES = 128
SUBLANES = 8
MXU_DIM = 256
VMEM_LIMIT = 48 * 1024 * 1024


def _dot(a, b):
    return jnp.dot(a, b, preferred_element_type=F32)


def _sigmoid(x):
    return 1.0 / (1.0 + jnp.exp(-x))


def _softplus(x):
    return jnp.maximum(x, 0.0) + jnp.log1p(jnp.exp(-jnp.abs(x)))


def _rms(x, g):
    return x * lax.rsqrt(jnp.mean(x * x, axis=-1, keepdims=True) + EPS) * g


def _head_sums(x):
    lo = lax.broadcasted_iota(jnp.int32, (x.shape[0], LANES), 1) < HEAD
    cols = []
    for i in range(0, x.shape[1], LANES):
        xi = x[:, i:i + LANES]
        s_lo = jnp.sum(jnp.where(lo, xi, 0.0), axis=-1, keepdims=True)
        s_hi = jnp.sum(xi, axis=-1, keepdims=True) - s_lo
        cols.append(jnp.where(lo, s_lo, s_hi))
    return cols[0] if len(cols) == 1 else jnp.concatenate(cols, axis=1)


def _sum_left(ones_bf, x):
    p1 = x.astype(BF16)
    p2 = (x - p1.astype(F32)).astype(BF16)
    return _dot(ones_bf, p1) + _dot(ones_bf, p2)


def _inproj_kernel(*refs, has_v1):
    if has_v1:
        x_ref, g_ref, w_ref, v1_ref, p_ref, hv_ref, h_scr = refs
    else:
        x_ref, g_ref, w_ref, p_ref, h_scr = refs

    @pl.when(pl.program_id(1) == 0)
    def _():
        hb = _rms(x_ref[...], g_ref[...]).astype(BF16)
        h_scr[...] = hb
        if has_v1:
            hv_ref[...] = _dot(hb, v1_ref[...])

    p_ref[...] = _dot(h_scr[...], w_ref[...]).astype(p_ref.dtype)


def _inproj(x2, g, w_bf, v1p_bf, tm, tn):
    T, D = x2.shape
    N = w_bf.shape[1]
    has_v1 = v1p_bf is not None
    in_specs = [pl.BlockSpec((tm, D), lambda i, j: (i, 0)),
                pl.BlockSpec((1, D), lambda i, j: (0, 0)),
                pl.BlockSpec((D, tn), lambda i, j: (0, j))]
    args = [x2, g, w_bf]
    out_shape = [jax.ShapeDtypeStruct((T, N), BF16)]
    out_specs = [pl.BlockSpec((tm, tn), lambda i, j: (i, j))]
    if has_v1:
        in_specs.append(pl.BlockSpec((D, LANES), lambda i, j: (0, 0)))
        args.append(v1p_bf)
        out_shape.append(jax.ShapeDtypeStruct((T, LANES), F32))
        out_specs.append(pl.BlockSpec((tm, LANES), lambda i, j: (i, 0)))
    res = pl.pallas_call(
        functools.partial(_inproj_kernel, has_v1=has_v1),
        grid=(T // tm, N // tn),
        in_specs=in_specs, out_specs=out_specs, out_shape=out_shape,
        scratch_shapes=[pltpu.VMEM((tm, D), BF16)],
        compiler_params=pltpu.CompilerParams(
            dimension_semantics=("parallel", "arbitrary"), vmem_limit_bytes=VMEM_LIMIT),
        name="inproj",
    )(*args)
    return (res[0], res[1]) if has_v1 else (res[0], None)


def _shifted(tail_ref, x, j, ts):
    w = x.shape[-1]
    x3 = x.reshape(ts // SUBLANES, SUBLANES, w)
    tail3 = tail_ref[...].reshape(1, SUBLANES, w)
    sub = lax.broadcasted_iota(jnp.int32, (1, SUBLANES, w), 1)
    out = []
    for d in range(1, j + 1):
        rot = pltpu.roll(x3, d, 1)
        prev_group = jnp.concatenate([pltpu.roll(tail3, d, 1), rot[:-1]], axis=0)
        out.append(jnp.where(sub < d, prev_group, rot).reshape(ts, w))
    tail_ref[...] = x[ts - SUBLANES:, :]
    return out


def _delayed(tail_ref, x_bf, x_f32, smat_ref, j, ts):
    allsh = _dot(smat_ref[0:j * ts, :], x_bf)
    tail = tail_ref[...]
    sub = lax.broadcasted_iota(jnp.int32, tail.shape, 0)
    out = []
    for d in range(1, j + 1):
        blk = allsh[(d - 1) * ts:d * ts]
        first = jnp.where(sub < d, pltpu.roll(tail, d, 0), blk[:SUBLANES])
        out.append(jnp.concatenate([first, blk[SUBLANES:]], axis=0))
    tail_ref[...] = x_f32[ts - SUBLANES:, :]
    return out


def _mix_ab_kernel(ba_ref, ca_ref, xa_ref, xb_ref, gb_ref, ga_ref, gbb_ref, smat_ref,
                   cwa_ref, cwb_ref, cbb_ref, wg_ref, bia_ref, bii_ref, ap_ref, mba_ref, mbb_ref,
                   o_ref, ctail, xatail, xbtail, hcar, *, ts):
    t = pl.program_id(1)
    D = o_ref.shape[-1]

    @pl.when(t == 0)
    def _():
        ctail[...] = jnp.zeros_like(ctail)
        xatail[...] = jnp.zeros_like(xatail)
        xbtail[...] = jnp.zeros_like(xbtail)
        hcar[...] = jnp.zeros_like(hcar)

    c_bf, xa_bf = ca_ref[...], xa_ref[...]
    c, xa = c_bf.astype(F32), xa_bf.astype(F32)
    c1, c2 = _delayed(ctail, c_bf, c, smat_ref, 2, ts)
    xa1, xa2 = _delayed(xatail, xa_bf, xa, smat_ref, 2, ts)
    cwa = cwa_ref[...]
    y_a = ba_ref[...].astype(F32) * (c * xa * cwa[2:3] + c1 * xa1 * cwa[1:2] + c2 * xa2 * cwa[0:1])

    xb_bf = xb_ref[...]
    xb = xb_bf.astype(F32)
    x1, x2, x3 = _delayed(xbtail, xb_bf, xb, smat_ref, 3, ts)
    cwb = cwb_ref[...]
    u = xb * cwb[3:4] + x1 * cwb[2:3] + x2 * cwb[1:2] + x3 * cwb[0:1] + cbb_ref[...]

    ub = u.astype(BF16)
    pre_a, pre_i = [], []
    for q in range(D // LANES):
        res = _dot(ub[:, q * LANES:(q + 1) * LANES], wg_ref[q])
        pre_a.append(res[:, :LANES])
        pre_i.append(res[:, LANES:])
    gate_a = _sigmoid(jnp.concatenate(pre_a, axis=-1) + bia_ref[...])
    gate_i = _sigmoid(jnp.concatenate(pre_i, axis=-1) + bii_ref[...])
    log_a = (-LRU_C) * gate_a * _softplus(ap_ref[...])
    a = jnp.exp(log_a)
    m2 = -jnp.tanh(log_a) * (1.0 + a * a)
    mult = jnp.where(m2 > 0.0, m2 * lax.rsqrt(m2), 0.0)
    row = lax.broadcasted_iota(jnp.int32, (ts, D), 0)
    mult = jnp.where((row == 0) & (t == 0), 1.0, mult)
    h = u * gate_i * mult

    ng = ts // SUBLANES
    h = h.reshape(ng, SUBLANES, D)
    a = a.reshape(ng, SUBLANES, D)
    sub = lax.broadcasted_iota(jnp.int32, (1, SUBLANES, D), 1)
    d = 1
    while d < SUBLANES:
        keep = sub >= d
        h_sh = jnp.where(keep, pltpu.roll(h, d, 1), 0.0)
        a_sh = jnp.where(keep, pltpu.roll(a, d, 1), 1.0)
        h = h + a * h_sh
        a = a * a_sh
        d *= 2
    carry = hcar[...]
    groups = []
    for gi in range(ng):
        hgrp = h[gi] + a[gi] * carry
        groups.append(hgrp)
        carry = hgrp[SUBLANES - 1:SUBLANES]
    h = jnp.concatenate(groups, axis=0)
    hcar[...] = carry

    gb = gb_ref[...].astype(F32)
    gelu = 0.5 * gb * (1.0 + jnp.tanh(0.7978845608028654 * (gb + 0.044715 * gb * gb * gb)))
    y_b = h * gelu
    o_ref[...] = (_sigmoid(ga_ref[...].astype(F32) + mba_ref[...]) * y_a
                  + _sigmoid(gbb_ref[...].astype(F32) + mbb_ref[...]) * y_b)


def _mix_ab(p3, cwa, cwb, cbb, wg, bia, bii, ap, mba, mbb, ts):
    B, S, _ = p3.shape
    D = cwa.shape[-1]
    ri = jnp.arange(ts)[:, None]
    smat = jnp.concatenate([(jnp.arange(ts)[None, :] == ri - d) for d in (1, 2, 3)], axis=0).astype(BF16)

    def col(c):
        return pl.BlockSpec((None, ts, D), lambda b, t, c=c: (b, t, c))

    def par(shape):
        nd = len(shape)
        return pl.BlockSpec(shape, lambda b, t, nd=nd: (0,) * nd)

    return pl.pallas_call(
        functools.partial(_mix_ab_kernel, ts=ts),
        grid=(B, S // ts),
        in_specs=[col(0), col(1), col(2), col(3), col(4), col(5), col(6), par(smat.shape),
                  par(cwa.shape), par(cwb.shape), par(cbb.shape), par(wg.shape),
                  par(bia.shape), par(bii.shape), par(ap.shape), par(mba.shape), par(mbb.shape)],
        out_specs=pl.BlockSpec((None, ts, D), lambda b, t: (b, t, 0)),
        out_shape=jax.ShapeDtypeStruct((B, S, D), F32),
        scratch_shapes=[pltpu.VMEM((SUBLANES, D), F32)] * 3 + [pltpu.VMEM((1, D), F32)],
        compiler_params=pltpu.CompilerParams(
            dimension_semantics=("parallel", "arbitrary"), vmem_limit_bytes=VMEM_LIMIT),
        name="mix_ab",
    )(p3, p3, p3, p3, p3, p3, p3, smat, cwa, cwb, cbb, wg, bia, bii, ap, mba, mbb)


def _bmm(a, b):
    return jnp.einsum('nik,nkj->nij', a.astype(BF16), b.astype(BF16), preferred_element_type=F32)


def _bmm_nt(a, b):
    return jnp.einsum('nik,njk->nij', a.astype(BF16), b.astype(BF16), preferred_element_type=F32)


def _bmm_tn(a, b):
    return jnp.einsum('nti,ntj->nij', a.astype(BF16), b.astype(BF16), preferred_element_type=F32)


def _bd(x, bdmask):
    xb = x.astype(BF16)
    return jnp.where(bdmask, jnp.concatenate([xb, xb], axis=1), jnp.zeros((), BF16))


def _tri_inv(A, row, hcol, bdmask):
    def blk(s):
        return (row // s) == (hcol // s)

    def pmm(x, y):
        return _bmm(x, _bd(y, bdmask))

    L = A.shape[1]
    eye = (row == hcol).astype(F32)
    Ad = jnp.where(blk(SUBLANES), A, 0.0)
    T = eye + Ad
    P = Ad
    Pbd = _bd(P, bdmask)
    s = 2
    while s < SUBLANES:
        P = _bmm(P, Pbd)
        Pbd = _bd(P, bdmask)
        T = T + _bmm(T, Pbd)
        s *= 2
    s = SUBLANES
    while s < L:
        Aoff = jnp.where(blk(2 * s) & jnp.logical_not(blk(s)), A, 0.0)
        T = T + pmm(pmm(T, Aoff), T)
        s *= 2
    return T


def _rwkv_kernel(*refs, ts, hg, layer0):
    it = iter(refs)
    pr_ref, pk_ref, pv_ref, pwa_ref, pxg_ref, pgc_ref, mab_ref = [next(it) for _ in range(7)]
    if not layer0:
        hv_ref, vf_ref = next(it), next(it)
    (mur_ref, muk_ref, muv_ref, muwa_ref, mug_ref, w0_ref, a0_ref, kk_ref, ka_ref, rk_ref,
     lng_ref, lnb_ref, mbc_ref) = [next(it) for _ in range(13)]
    if not layer0:
        v0_ref = next(it)
    w2_ref, a2_ref, g2_ref = next(it), next(it), next(it)
    if not layer0:
        v2_ref = next(it)
    o_ref = next(it)
    if layer0:
        vfo_ref = next(it)
    rpad, kpad, vpad, wapad, xgpad, st_scr = [next(it) for _ in range(6)]

    t = pl.program_id(2)
    cw = hg * HEAD
    nc = ts // CHUNK
    npair = hg // 2
    L = CHUNK

    @pl.when(t == 0)
    def _():
        for pad in (rpad, kpad, vpad, wapad, xgpad):
            pad[...] = jnp.zeros_like(pad)
        st_scr[...] = jnp.zeros_like(st_scr)

    def token_shift(pad, x_ref, mu_ref):
        x = x_ref[...].astype(F32)
        (prev,) = _shifted(pad, x, 1, ts)
        return x + (prev - x) * mu_ref[...]

    r = token_shift(rpad, pr_ref, mur_ref)
    k = token_shift(kpad, pk_ref, muk_ref)
    v = token_shift(vpad, pv_ref, muv_ref)
    xwa = token_shift(wapad, pwa_ref, muwa_ref)
    xg = token_shift(xgpad, pxg_ref, mug_ref)

    wl = w0_ref[...] + _dot(jnp.tanh(xwa).astype(BF16), w2_ref[...])
    ld = -EXP_M05 * _sigmoid(wl)
    a = _sigmoid(a0_ref[...] + _dot(xwa.astype(BF16), a2_ref[...]))
    g = _dot(_sigmoid(xg).astype(BF16), g2_ref[...])
    if layer0:
        vfo_ref[...] = v
    else:
        mix = _sigmoid(v0_ref[...] + _dot(hv_ref[...].astype(BF16), v2_ref[...]))
        v = v + (vf_ref[...] - v) * mix

    ri = lax.broadcasted_iota(jnp.int32, (ts, ts), 0)
    ci = lax.broadcasted_iota(jnp.int32, (ts, ts), 1)
    tri = jnp.where(((ri // L) == (ci // L)) & (ci <= ri), 1.0, 0.0).astype(BF16)

    c = _sum_left(tri, ld)
    e_c = jnp.exp(c)
    e_n = jnp.exp(-c)
    e_cl = jnp.broadcast_to(e_c.reshape(nc, L, cw)[:, L - 1:L, :], (nc, L, cw)).reshape(ts, cw)

    kk = k * kk_ref[...]
    kk = kk * lax.rsqrt(jnp.maximum(_head_sums(kk * kk), 1e-24))
    k = k * (1.0 + (a - 1.0) * ka_ref[...])
    bt = kk * a * e_n
    kt = k * e_n
    Rt_f = (r * e_c).astype(BF16)
    At_f = (-kk * jnp.exp(c - ld)).astype(BF16)
    Bt_f = bt.astype(BF16)
    Kt_f = kt.astype(BF16)
    Bh_f = (bt * e_cl).astype(BF16)
    Kh_f = (kt * e_cl).astype(BF16)
    v_b = v.astype(BF16)

    PW = 2 * HEAD
    prow = lax.broadcasted_iota(jnp.int32, (L, PW), 0)
    hcol = lax.broadcasted_iota(jnp.int32, (L, PW), 1) % HEAD
    low_incl = hcol <= prow
    low_strict = hcol < prow
    brow = lax.broadcasted_iota(jnp.int32, (2 * L, PW), 0)
    bcol = lax.broadcasted_iota(jnp.int32, (2 * L, PW), 1)
    bdmask = (brow // L) == (bcol // HEAD)
    eye2 = brow == bcol

    def pairs(x):
        per_pair = [x[:, p * PW:(p + 1) * PW].reshape(nc, 1, L, PW) for p in range(npair)]
        return jnp.concatenate(per_pair, axis=1).reshape(nc * npair, L, PW)

    Rt, At, Bt, Kt, Bh, Kh, vp = [pairs(x) for x in (Rt_f, At_f, Bt_f, Kt_f, Bh_f, Kh_f, v_b)]
    zero_p = jnp.zeros_like(vp)
    AA = _bmm_nt(jnp.concatenate([Rt, At], axis=1),
                 jnp.concatenate([_bd(Bt, bdmask), _bd(Kt, bdmask)], axis=1))
    Arb = jnp.where(low_incl, AA[:, :L, :PW], 0.0)
    Ark = jnp.where(low_incl, AA[:, :L, PW:], 0.0)
    Aab = jnp.where(low_strict, AA[:, L:, :PW], 0.0)
    Aak = jnp.where(low_strict, AA[:, L:, PW:], 0.0)
    T = _tri_inv(Aab, prow, hcol, bdmask)
    AakV = _bmm(Aak, _bd(vp, bdmask))
    WU = _bmm(T, jnp.concatenate([_bd(At, bdmask), _bd(AakV, bdmask)], axis=2))
    W, U0 = WU[:, :, :PW], WU[:, :, PW:]
    QY = _bmm(jnp.concatenate([Arb, Ark], axis=2),
              jnp.concatenate([jnp.concatenate([_bd(W, bdmask), _bd(U0, bdmask)], axis=2),
                               jnp.concatenate([jnp.zeros((nc * npair, 2 * L, PW), BF16), _bd(vp, bdmask)],
                                               axis=2)], axis=1))
    Q = Rt.astype(F32) + QY[:, :, :PW]
    Y0 = QY[:, :, PW:]
    MG = _bmm_tn(jnp.concatenate([Bh, Kh], axis=1),
                 jnp.concatenate([WU.astype(BF16), jnp.concatenate([zero_p, vp], axis=2)], axis=1))
    pl_row = pairs(e_c)[:, L - 1:L, :]
    M = jnp.where(bdmask, MG[:, :, :PW], 0.0) + jnp.where(eye2, pl_row, 0.0)
    G = jnp.where(bdmask, MG[:, :, PW:], 0.0)
    QM = jnp.concatenate([Q, M], axis=1).astype(BF16)

    H = st_scr[...]
    ys = []
    for ch in range(nc):
        sl = slice(ch * npair, (ch + 1) * npair)
        YH = _bmm(QM[sl], H)
        ys.append(YH[:, :L, :] + Y0[sl])
        H = YH[:, L:, :] + G[sl]
    st_scr[...] = H
    y = jnp.concatenate(
        [jnp.concatenate([ys[ch][p] for ch in range(nc)], axis=0) for p in range(npair)], axis=-1)
    inv_n = 1.0 / HEAD
    yc = y - _head_sums(y) * inv_n
    var = _head_sums(yc * yc) * inv_n
    yn = yc * lax.rsqrt(var + LNX_EPS)
    bonus = _head_sums(r * k * rk_ref[...])
    y_c = (yn * lng_ref[...] + lnb_ref[...] + bonus * v) * g
    m = mab_ref[...] + _sigmoid(pgc_ref[...].astype(F32) + mbc_ref[...]) * y_c
    o_ref[...] = m.astype(o_ref.dtype)


def _rwkv(p3, hv3, vf3, mab3, mu, w0, a0, kkp, kap, rkp, lng, lnb, mbc, v0, w2p, a2p, g2, v2p,
          col_r, col_k, col_v, col_wa, col_xg, col_gc, ts, hg):
    B, S, _ = p3.shape
    D = w0.shape[-1]
    cw = hg * HEAD
    layer0 = vf3 is None

    def colblk(c0, w, grouped):
        return pl.BlockSpec((None, ts, w), lambda b, gi, t: (b, t, c0 // w + (gi if grouped else 0)))

    def rowpar(c0, w, grouped):
        return pl.BlockSpec((1, w), lambda b, gi, t: (0, c0 // w + (gi if grouped else 0)))

    def matpar(rows):
        return pl.BlockSpec((rows, cw), lambda b, gi, t: (0, gi))

    act = pl.BlockSpec((None, ts, cw), lambda b, gi, t: (b, t, gi))
    in_specs = [colblk(col_r, cw, True), colblk(col_k, cw, True), colblk(col_v, cw, True),
                colblk(col_wa, LANES, False), colblk(col_xg, LANES, False), colblk(col_gc, cw, True), act]
    args = [p3, p3, p3, p3, p3, p3, mab3]
    if not layer0:
        in_specs += [pl.BlockSpec((None, ts, LANES), lambda b, gi, t: (b, t, 0)), act]
        args += [hv3, vf3]
    in_specs += [rowpar(0, cw, True), rowpar(D, cw, True), rowpar(2 * D, cw, True),
                 rowpar(3 * D, LANES, False), rowpar(3 * D + LANES, LANES, False)]
    args += [mu] * 5
    in_specs += [rowpar(0, cw, True)] * 8
    args += [w0, a0, kkp, kap, rkp, lng, lnb, mbc]
    if not layer0:
        in_specs.append(rowpar(0, cw, True))
        args.append(v0)
    in_specs += [matpar(LANES), matpar(LANES), matpar(LANES)]
    args += [w2p, a2p, g2]
    if not layer0:
        in_specs.append(matpar(LANES))
        args.append(v2p)
    out_shape = [jax.ShapeDtypeStruct((B, S, D), BF16)]
    out_specs = [act]
    if layer0:
        out_shape.append(jax.ShapeDtypeStruct((B, S, D), F32))
        out_specs.append(act)
    res = pl.pallas_call(
        functools.partial(_rwkv_kernel, ts=ts, hg=hg, layer0=layer0),
        grid=(B, D // cw, S // ts),
        in_specs=in_specs, out_specs=out_specs, out_shape=out_shape,
        scratch_shapes=[pltpu.VMEM((SUBLANES, cw), F32)] * 3
        + [pltpu.VMEM((SUBLANES, LANES), F32)] * 2
        + [pltpu.VMEM((hg // 2, 2 * HEAD, 2 * HEAD), F32)],
        compiler_params=pltpu.CompilerParams(
            dimension_semantics=("parallel", "parallel", "arbitrary"), vmem_limit_bytes=VMEM_LIMIT),
        name="rwkv",
    )(*args)
    return (res[0], res[1]) if layer0 else (res[0], vf3)


def _mlp_kernel(x_ref, m_ref, wo_ref, g2_ref, w1_ref, w2_ref, fg_ref, o_ref, acc, h2_scr, *, final):
    j = pl.program_id(1)

    @pl.when(j == 0)
    def _():
        xn = x_ref[...] + _dot(m_ref[...], wo_ref[...])
        acc[...] = xn
        h2_scr[...] = _rms(xn, g2_ref[...]).astype(BF16)

    hid = jnp.square(jnp.maximum(_dot(h2_scr[...], w1_ref[...]), 0.0))
    acc[...] += _dot(hid.astype(BF16), w2_ref[...])

    @pl.when(j == pl.num_programs(1) - 1)
    def _():
        o_ref[...] = _rms(acc[...], fg_ref[...]) if final else acc[...]


def _mlp(x2, m2, wo_bf, g2, w1_bf, w2_bf, fg, tm, tf, final):
    T, D = x2.shape
    FF = w1_bf.shape[1]
    return pl.pallas_call(
        functools.partial(_mlp_kernel, final=final),
        grid=(T // tm, FF // tf),
        in_specs=[pl.BlockSpec((tm, D), lambda i, j: (i, 0)),
                  pl.BlockSpec((tm, D), lambda i, j: (i, 0)),
                  pl.BlockSpec((D, D), lambda i, j: (0, 0)),
                  pl.BlockSpec((1, D), lambda i, j: (0, 0)),
                  pl.BlockSpec((D, tf), lambda i, j: (0, j)),
                  pl.BlockSpec((tf, D), lambda i, j: (j, 0)),
                  pl.BlockSpec((1, D), lambda i, j: (0, 0))],
        out_specs=pl.BlockSpec((tm, D), lambda i, j: (i, 0)),
        out_shape=jax.ShapeDtypeStruct((T, D), F32),
        scratch_shapes=[pltpu.VMEM((tm, D), F32), pltpu.VMEM((tm, D), BF16)],
        compiler_params=pltpu.CompilerParams(
            dimension_semantics=("parallel", "arbitrary"), vmem_limit_bytes=VMEM_LIMIT),
        name="mlp",
    )(x2, m2, wo_bf, g2, w1_bf, w2_bf, fg)


def _tile(n, pref):
    t = min(n, pref)
    while n % t:
        t //= 2
    return t


def _col_tile(n, pref):
    best = LANES
    for m in range(1, n // LANES + 1):
        if (n // LANES) % m == 0 and m * LANES <= pref:
            best = m * LANES
    return best


def kernel(x, norm1_g, w_in, merge_b, conv_a_w, lru_conv_w, lru_conv_b, lru_wa, lru_ba, lru_wi, lru_bi,
           lru_a_param, rwkv_mu, rwkv_w0, rwkv_w2, rwkv_a0, rwkv_a2, rwkv_g2, rwkv_kk, rwkv_ka, rwkv_rk,
           rwkv_lnx_g, rwkv_lnx_b, rwkv_v0, rwkv_v1, rwkv_v2, w_out, norm2_g, mlp_w1, mlp_w2, final_g):
    B, S, D = x.shape
    depth = w_in.shape[0]
    T = B * S
    n_in = w_in.shape[-1]
    col_gate = 3 * D + 2 * D
    col_c = col_gate + 3 * D
    col_wa = col_c + 3 * D
    col_xg = col_wa + R_W + R_A
    assert n_in == col_xg + R_G and R_W + R_A == LANES and R_G == LANES and D % LANES == 0

    tm = _tile(T, 2048)
    tn = _col_tile(n_in, 1280)
    ts_ab = _tile(S, 256)
    ts_c = _tile(S, 256)
    hg = 16
    tm_mlp = _tile(T, 512)
    tf = _tile(mlp_w1.shape[-1], 1024)

    row = lambda vec: vec.reshape(1, -1)
    x2 = x.reshape(T, D)
    vf3 = None
    for l in range(depth):
        w_bf = w_in[l].astype(BF16)
        v1p = v2p = None
        if l > 0:
            v1p = jnp.pad(rwkv_v1[l - 1], ((0, 0), (0, LANES - R_V))).astype(BF16)
            v2p = jnp.pad(rwkv_v2[l - 1], ((0, LANES - R_V), (0, 0))).astype(BF16)
        w2p = jnp.pad(rwkv_w2[l], ((0, R_A), (0, 0))).astype(BF16)
        a2p = jnp.pad(rwkv_a2[l], ((R_W, 0), (0, 0))).astype(BF16)
        g2 = rwkv_g2[l].astype(BF16)
        bw = D // LRU_HEADS
        wa = lru_wa[l].reshape(LRU_HEADS // 2, 2, bw, bw)
        wi = lru_wi[l].reshape(LRU_HEADS // 2, 2, bw, bw)
        z = jnp.zeros_like(wa[:, 0])
        blockdiag = lambda w: jnp.concatenate(
            [jnp.concatenate([w[:, 0], z], axis=-1), jnp.concatenate([z, w[:, 1]], axis=-1)], axis=-2)
        wg = jnp.concatenate([blockdiag(wa), blockdiag(wi)], axis=-1).astype(BF16)
        mb = merge_b[l]

        p2, hv2 = _inproj(x2, row(norm1_g[l]), w_bf, v1p, tm, tn)
        p3 = p2.reshape(B, S, n_in)
        mab3 = _mix_ab(p3, conv_a_w[l], lru_conv_w[l], row(lru_conv_b[l]), wg, row(lru_ba[l]), row(lru_bi[l]),
                       row(lru_a_param[l]), row(mb[:D]), row(mb[D:2 * D]), ts_ab)
        m3, vf3 = _rwkv(p3, None if l == 0 else hv2.reshape(B, S, LANES), vf3, mab3,
                        row(rwkv_mu[l]), row(rwkv_w0[l]), row(rwkv_a0[l]), row(rwkv_kk[l]), row(rwkv_ka[l]),
                        row(rwkv_rk[l]), row(rwkv_lnx_g[l]), row(rwkv_lnx_b[l]), row(mb[2 * D:]),
                        None if l == 0 else row(rwkv_v0[l - 1]), w2p, a2p, g2, v2p,
                        col_c, col_c + D, col_c + 2 * D, col_wa, col_xg, col_gate + 2 * D, ts_c, hg)
        x2 = _mlp(x2, m3.reshape(T, D), w_out[l].astype(BF16), row(norm2_g[l]), mlp_w1[l].astype(BF16),
                  mlp_w2[l].astype(BF16), row(final_g), tm_mlp, tf, final=(l == depth - 1))
    return x2.reshape(B, S, D)
```

```python
import functools

import jax
import jax.numpy as jnp
from jax import lax
from jax.experimental import pallas as pl
from jax.experimental.pallas import tpu as pltpu

F32 = jnp.float32
BF16 = jnp.bfloat16

EPS = 1e-6
LRU_C = 8.0
LRU_HEADS = 16
HEAD = 64
CHUNK = 64
LNX_EPS = HEAD * 1e-5
EXP_M05 = 0.6065306597126334
LOG2E = 1.4426950408889634
R_W, R_A, R_G, R_V = 64, 64, 128, 32
LANES = 128
SUBLANES = 8
MXU_DIM = 256
VMEM_LIMIT = 48 * 1024 * 1024


def _dot(a, b):
    return jnp.dot(a, b, preferred_element_type=F32)


def _sigmoid(x):
    return 1.0 / (1.0 + jnp.exp(-x))


def _softplus(x):
    return jnp.maximum(x, 0.0) + jnp.log1p(jnp.exp(-jnp.abs(x)))


def _rms(x, g):
    return x * lax.rsqrt(jnp.mean(x * x, axis=-1, keepdims=True) + EPS) * g


def _head_sums(x):
    lo = lax.broadcasted_iota(jnp.int32, (x.shape[0], LANES), 1) < HEAD
    cols = []
    for i in range(0, x.shape[1], LANES):
        xi = x[:, i:i + LANES]
        s_lo = jnp.sum(jnp.where(lo, xi, 0.0), axis=-1, keepdims=True)
        s_hi = jnp.sum(xi, axis=-1, keepdims=True) - s_lo
        cols.append(jnp.where(lo, s_lo, s_hi))
    return cols[0] if len(cols) == 1 else jnp.concatenate(cols, axis=1)


def _sum_left(ones_bf, x):
    p1 = x.astype(BF16)
    p2 = (x - p1.astype(F32)).astype(BF16)
    return _dot(ones_bf, p1) + _dot(ones_bf, p2)


def _inproj_kernel(*refs, has_v1):
    if has_v1:
        x_ref, g_ref, w_ref, v1_ref, p_ref, hv_ref, h_scr = refs
    else:
        x_ref, g_ref, w_ref, p_ref, h_scr = refs

    @pl.when(pl.program_id(1) == 0)
    def _():
        hb = _rms(x_ref[...], g_ref[...]).astype(BF16)
        h_scr[...] = hb
        if has_v1:
            hv_ref[...] = _dot(hb, v1_ref[...])

    p_ref[...] = _dot(h_scr[...], w_ref[...]).astype(p_ref.dtype)


def _inproj(x2, g, w_bf, layer, v1p_bf, tm, tn):
    T, D = x2.shape
    N = w_bf.shape[-1]
    has_v1 = v1p_bf is not None
    in_specs = [pl.BlockSpec((tm, D), lambda i, j: (i, 0)),
                pl.BlockSpec((1, D), lambda i, j: (0, 0)),
                pl.BlockSpec((None, D, tn), lambda i, j: (layer, 0, j))]
    args = [x2, g, w_bf]
    out_shape = [jax.ShapeDtypeStruct((T, N), BF16)]
    out_specs = [pl.BlockSpec((tm, tn), lambda i, j: (i, j))]
    if has_v1:
        in_specs.append(pl.BlockSpec((D, LANES), lambda i, j: (0, 0)))
        args.append(v1p_bf)
        out_shape.append(jax.ShapeDtypeStruct((T, LANES), F32))
        out_specs.append(pl.BlockSpec((tm, LANES), lambda i, j: (i, 0)))
    res = pl.pallas_call(
        functools.partial(_inproj_kernel, has_v1=has_v1),
        grid=(T // tm, N // tn),
        in_specs=in_specs, out_specs=out_specs, out_shape=out_shape,
        scratch_shapes=[pltpu.VMEM((tm, D), BF16)],
        compiler_params=pltpu.CompilerParams(
            dimension_semantics=("parallel", "arbitrary"), vmem_limit_bytes=VMEM_LIMIT),
        name="inproj",
    )(*args)
    return (res[0], res[1]) if has_v1 else (res[0], None)


def _shifted(tail_ref, x, j, ts):
    w = x.shape[-1]
    x3 = x.reshape(ts // SUBLANES, SUBLANES, w)
    tail3 = tail_ref[...].reshape(1, SUBLANES, w)
    sub = lax.broadcasted_iota(jnp.int32, (1, SUBLANES, w), 1)
    out = []
    for d in range(1, j + 1):
        rot = pltpu.roll(x3, d, 1)
        prev_group = jnp.concatenate([pltpu.roll(tail3, d, 1), rot[:-1]], axis=0)
        out.append(jnp.where(sub < d, prev_group, rot).reshape(ts, w))
    tail_ref[...] = x[ts - SUBLANES:, :]
    return out


def _delayed(tail_ref, x_bf, x_f32, smat_ref, j, ts):
    allsh = _dot(smat_ref[0:j * ts, :], x_bf)
    tail = tail_ref[...]
    sub = lax.broadcasted_iota(jnp.int32, tail.shape, 0)
    out = []
    for d in range(1, j + 1):
        blk = allsh[(d - 1) * ts:d * ts]
        first = jnp.where(sub < d, pltpu.roll(tail, d, 0), blk[:SUBLANES])
        out.append(jnp.concatenate([first, blk[SUBLANES:]], axis=0))
    tail_ref[...] = x_f32[ts - SUBLANES:, :]
    return out


def _mix_ab_kernel(ba_ref, ca_ref, xa_ref, xb_ref, gb_ref, ga_ref, gbb_ref, smat_ref,
                   cwa_ref, cwb_ref, cbb_ref, wg_ref, bia_ref, bii_ref, ap_ref, mba_ref, mbb_ref,
                   o_ref, ctail, xatail, xbtail, hcar, hs_scr, as_scr, ent_scr, *, ts):
    t = pl.program_id(1)
    D = o_ref.shape[-1]

    @pl.when(t == 0)
    def _():
        ctail[...] = jnp.zeros_like(ctail)
        xatail[...] = jnp.zeros_like(xatail)
        xbtail[...] = jnp.zeros_like(xbtail)
        hcar[...] = jnp.zeros_like(hcar)

    c_bf, xa_bf = ca_ref[...], xa_ref[...]
    c, xa = c_bf.astype(F32), xa_bf.astype(F32)
    c1, c2 = _delayed(ctail, c_bf, c, smat_ref, 2, ts)
    xa1, xa2 = _delayed(xatail, xa_bf, xa, smat_ref, 2, ts)
    cwa = cwa_ref[...]
    y_a = ba_ref[...].astype(F32) * (c * xa * cwa[2:3] + c1 * xa1 * cwa[1:2] + c2 * xa2 * cwa[0:1])

    xb_bf = xb_ref[...]
    xb = xb_bf.astype(F32)
    x1, x2, x3 = _delayed(xbtail, xb_bf, xb, smat_ref, 3, ts)
    cwb = cwb_ref[...]
    u = xb * cwb[3:4] + x1 * cwb[2:3] + x2 * cwb[1:2] + x3 * cwb[0:1] + cbb_ref[...]

    ub = u.astype(BF16)
    pre_a, pre_i = [], []
    for q in range(D // LANES):
        res = _dot(ub[:, q * LANES:(q + 1) * LANES], wg_ref[q])
        pre_a.append(res[:, :LANES])
        pre_i.append(res[:, LANES:])
    gate_a = _sigmoid(jnp.concatenate(pre_a, axis=-1) + bia_ref[...])
    gate_i = _sigmoid(jnp.concatenate(pre_i, axis=-1) + bii_ref[...])
    log_a = (-LRU_C) * gate_a * _softplus(ap_ref[...])
    a = jnp.exp(log_a)
    m2 = -jnp.tanh(log_a) * (1.0 + a * a)
    mult = jnp.where(m2 > 0.0, m2 * lax.rsqrt(m2), 0.0)
    row = lax.broadcasted_iota(jnp.int32, (ts, D), 0)
    mult = jnp.where((row == 0) & (t == 0), 1.0, mult)
    h = u * gate_i * mult

    ng = ts // SUBLANES
    nq = D // LANES

    for q in range(nq):
        lanes = slice(q * LANES, (q + 1) * LANES)
        hs_scr[q] = h[:, lanes]
        as_scr[q] = a[:, lanes]
        cum_a, loc_h = [], []
        for s in range(SUBLANES):
            a_s = as_scr[q, pl.ds(s, ng, stride=SUBLANES), :]
            u_s = hs_scr[q, pl.ds(s, ng, stride=SUBLANES), :]
            cum_a.append(a_s * cum_a[-1] if s else a_s)
            loc_h.append(a_s * loc_h[-1] + u_s if s else u_s)
        carry = hcar[:, lanes]
        for gi in range(ng):
            ent_scr[gi:gi + 1, lanes] = carry
            carry = loc_h[-1][gi:gi + 1] + cum_a[-1][gi:gi + 1] * carry
        hcar[:, lanes] = carry
        enter = ent_scr[:, lanes]
        for s in range(SUBLANES):
            hs_scr[q, pl.ds(s, ng, stride=SUBLANES), :] = loc_h[s] + cum_a[s] * enter
    h = jnp.concatenate([hs_scr[q] for q in range(nq)], axis=-1)

    gb = gb_ref[...].astype(F32)
    gelu = 0.5 * gb * (1.0 + jnp.tanh(0.7978845608028654 * (gb + 0.044715 * gb * gb * gb)))
    y_b = h * gelu
    o_ref[...] = (_sigmoid(ga_ref[...].astype(F32) + mba_ref[...]) * y_a
                  + _sigmoid(gbb_ref[...].astype(F32) + mbb_ref[...]) * y_b)


def _mix_ab(p3, cwa, cwb, cbb, wg, bia, bii, ap, mba, mbb, ts):
    B, S, _ = p3.shape
    D = cwa.shape[-1]
    ri = jnp.arange(ts)[:, None]
    smat = jnp.concatenate([(jnp.arange(ts)[None, :] == ri - d) for d in (1, 2, 3)], axis=0).astype(BF16)

    def col(c):
        return pl.BlockSpec((None, ts, D), lambda b, t, c=c: (b, t, c))

    def par(shape):
        nd = len(shape)
        return pl.BlockSpec(shape, lambda b, t, nd=nd: (0,) * nd)

    return pl.pallas_call(
        functools.partial(_mix_ab_kernel, ts=ts),
        grid=(B, S // ts),
        in_specs=[col(0), col(1), col(2), col(3), col(4), col(5), col(6), par(smat.shape),
                  par(cwa.shape), par(cwb.shape), par(cbb.shape), par(wg.shape),
                  par(bia.shape), par(bii.shape), par(ap.shape), par(mba.shape), par(mbb.shape)],
        out_specs=pl.BlockSpec((None, ts, D), lambda b, t: (b, t, 0)),
        out_shape=jax.ShapeDtypeStruct((B, S, D), F32),
        scratch_shapes=[pltpu.VMEM((SUBLANES, D), F32)] * 3 + [pltpu.VMEM((1, D), F32)]
        + [pltpu.VMEM((D // LANES, ts, LANES), F32)] * 2 + [pltpu.VMEM((ts // SUBLANES, D), F32)],
        compiler_params=pltpu.CompilerParams(
            dimension_semantics=("parallel", "arbitrary"), vmem_limit_bytes=VMEM_LIMIT),
        name="mix_ab",
    )(p3, p3, p3, p3, p3, p3, p3, smat, cwa, cwb, cbb, wg, bia, bii, ap, mba, mbb)


def _bmm(a, b):
    return jnp.einsum('nik,nkj->nij', a.astype(BF16), b.astype(BF16), preferred_element_type=F32)


def _bmm_nt(a, b):
    return jnp.einsum('nik,njk->nij', a.astype(BF16), b.astype(BF16), preferred_element_type=F32)


def _bmm_tn(a, b):
    return jnp.einsum('nti,ntj->nij', a.astype(BF16), b.astype(BF16), preferred_element_type=F32)


def _bd(x, bdmask):
    xb = x.astype(BF16)
    return jnp.where(bdmask, jnp.concatenate([xb, xb], axis=1), jnp.zeros((), BF16))


def _tri_inv(A, row, hcol, bdmask):
    def blk(s):
        return (row // s) == (hcol // s)

    def pmm(x, y):
        return _bmm(x, _bd(y, bdmask))

    L = A.shape[1]
    eye = (row == hcol).astype(F32)
    Ad = jnp.where(blk(SUBLANES), A, 0.0)
    T = eye + Ad
    P = Ad
    Pbd = _bd(P, bdmask)
    s = 2
    while s < SUBLANES:
        P = _bmm(P, Pbd)
        Pbd = _bd(P, bdmask)
        T = T + _bmm(T, Pbd)
        s *= 2
    s = SUBLANES
    while s < L:
        Aoff = jnp.where(blk(2 * s) & jnp.logical_not(blk(s)), A, 0.0)
        T = T + pmm(pmm(T, Aoff), T)
        s *= 2
    return T


def _rwkv_kernel(*refs, ts, hg, layer0):
    it = iter(refs)
    pr_ref, pk_ref, pv_ref, pwa_ref, pxg_ref, pgc_ref, mab_ref = [next(it) for _ in range(7)]
    if not layer0:
        hv_ref, vf_ref = next(it), next(it)
    (mur_ref, muk_ref, muv_ref, muwa_ref, mug_ref, w0_ref, a0_ref, kk_ref, ka_ref, rk_ref,
     lng_ref, lnb_ref, mbc_ref) = [next(it) for _ in range(13)]
    if not layer0:
        v0_ref = next(it)
    w2_ref, a2_ref, g2_ref = next(it), next(it), next(it)
    if not layer0:
        v2_ref = next(it)
    o_ref = next(it)
    if layer0:
        vfo_ref = next(it)
    rpad, kpad, vpad, wapad, xgpad, st_scr = [next(it) for _ in range(6)]

    t = pl.program_id(2)
    cw = hg * HEAD
    nc = ts // CHUNK
    npair = hg // 2
    L = CHUNK

    @pl.when(t == 0)
    def _():
        for pad in (rpad, kpad, vpad, wapad, xgpad):
            pad[...] = jnp.zeros_like(pad)
        st_scr[...] = jnp.zeros_like(st_scr)

    ri = lax.broadcasted_iota(jnp.int32, (ts, ts), 0)
    ci = lax.broadcasted_iota(jnp.int32, (ts, ts), 1)

    def token_shift(pad, x_ref, mu_ref):
        x = x_ref[...].astype(F32)
        (prev,) = _shifted(pad, x, 1, ts)
        return x + (prev - x) * mu_ref[...]

    r = token_shift(rpad, pr_ref, mur_ref)
    k = token_shift(kpad, pk_ref, muk_ref)
    v = token_shift(vpad, pv_ref, muv_ref)
    xwa = token_shift(wapad, pwa_ref, muwa_ref)
    xg = token_shift(xgpad, pxg_ref, mug_ref)

    wl = w0_ref[...] + _dot(jnp.tanh(xwa).astype(BF16), w2_ref[...])
    ld = (-LOG2E * EXP_M05) * _sigmoid(wl)
    a = _sigmoid(a0_ref[...] + _dot(xwa.astype(BF16), a2_ref[...]))
    g = _dot(_sigmoid(xg).astype(BF16), g2_ref[...])
    if layer0:
        vfo_ref[...] = v
    else:
        mix = _sigmoid(v0_ref[...] + _dot(hv_ref[...].astype(BF16), v2_ref[...]))
        v = v + (vf_ref[...] - v) * mix

    ri = lax.broadcasted_iota(jnp.int32, (ts, ts), 0)
    ci = lax.broadcasted_iota(jnp.int32, (ts, ts), 1)
    tri = jnp.where(((ri // L) == (ci // L)) & (ci <= ri), 1.0, 0.0).astype(BF16)

    c = _sum_left(tri, ld)
    e_c = jnp.exp2(c)
    e_n = 1.0 / e_c
    e_cl = jnp.broadcast_to(e_c.reshape(nc, L, cw)[:, L - 1:L, :], (nc, L, cw)).reshape(ts, cw)

    kk = k * kk_ref[...]
    kk = kk * lax.rsqrt(jnp.maximum(_head_sums(kk * kk), 1e-24))
    k = k * (1.0 + (a - 1.0) * ka_ref[...])
    bt = kk * a * e_n
    kt = k * e_n
    Rt_f = (r * e_c).astype(BF16)
    At_f = (-kk * jnp.exp2(c - ld)).astype(BF16)
    Bt_f = bt.astype(BF16)
    Kt_f = kt.astype(BF16)
    Bh_f = (bt * e_cl).astype(BF16)
    Kh_f = (kt * e_cl).astype(BF16)
    v_b = v.astype(BF16)

    PW = 2 * HEAD
    prow = lax.broadcasted_iota(jnp.int32, (L, PW), 0)
    hcol = lax.broadcasted_iota(jnp.int32, (L, PW), 1) % HEAD
    low_incl = hcol <= prow
    low_strict = hcol < prow
    brow = lax.broadcasted_iota(jnp.int32, (2 * L, PW), 0)
    bcol = lax.broadcasted_iota(jnp.int32, (2 * L, PW), 1)
    bdmask = (brow // L) == (bcol // HEAD)
    eye2 = brow == bcol

    def pairs(x):
        per_pair = [x[:, p * PW:(p + 1) * PW].reshape(nc, 1, L, PW) for p in range(npair)]
        return jnp.concatenate(per_pair, axis=1).reshape(nc * npair, L, PW)

    Rt, At, Bt, Kt, Bh, Kh, vp = [pairs(x) for x in (Rt_f, At_f, Bt_f, Kt_f, Bh_f, Kh_f, v_b)]
    zero_p = jnp.zeros_like(vp)
    AA = _bmm_nt(jnp.concatenate([Rt, At], axis=1),
                 jnp.concatenate([_bd(Bt, bdmask), _bd(Kt, bdmask)], axis=1))
    Arb = jnp.where(low_incl, AA[:, :L, :PW], 0.0)
    Ark = jnp.where(low_incl, AA[:, :L, PW:], 0.0)
    Aab = jnp.where(low_strict, AA[:, L:, :PW], 0.0)
    Aak = jnp.where(low_strict, AA[:, L:, PW:], 0.0)
    T = _tri_inv(Aab, prow, hcol, bdmask)
    AakV = _bmm(Aak, _bd(vp, bdmask))
    WU = _bmm(T, jnp.concatenate([_bd(At, bdmask), _bd(AakV, bdmask)], axis=2))
    W, U0 = WU[:, :, :PW], WU[:, :, PW:]
    QY = _bmm(jnp.concatenate([Arb, Ark], axis=2),
              jnp.concatenate([jnp.concatenate([_bd(W, bdmask), _bd(U0, bdmask)], axis=2),
                               jnp.concatenate([jnp.zeros((nc * npair, 2 * L, PW), BF16), _bd(vp, bdmask)],
                                               axis=2)], axis=1))
    Q = Rt.astype(F32) + QY[:, :, :PW]
    Y0 = QY[:, :, PW:]
    MG = _bmm_tn(jnp.concatenate([Bh, Kh], axis=1),
                 jnp.concatenate([WU.astype(BF16), jnp.concatenate([zero_p, vp], axis=2)], axis=1))
    pl_row = pairs(e_c)[:, L - 1:L, :]
    M = jnp.where(bdmask, MG[:, :, :PW], 0.0) + jnp.where(eye2, pl_row, 0.0)
    G = jnp.where(bdmask, MG[:, :, PW:], 0.0)
    QM = jnp.concatenate([Q, M], axis=1).astype(BF16)

    H = st_scr[...]
    ys = []
    for ch in range(nc):
        sl = slice(ch * npair, (ch + 1) * npair)
        YH = _bmm(QM[sl], H)
        ys.append(YH[:, :L, :] + Y0[sl])
        H = YH[:, L:, :] + G[sl]
    st_scr[...] = H
    y = jnp.concatenate(
        [jnp.concatenate([ys[ch][p] for ch in range(nc)], axis=0) for p in range(npair)], axis=-1)
    inv_n = 1.0 / HEAD
    yc = y - _head_sums(y) * inv_n
    var = _head_sums(yc * yc) * inv_n
    yn = yc * lax.rsqrt(var + LNX_EPS)
    bonus = _head_sums(r * k * rk_ref[...])
    y_c = (yn * lng_ref[...] + lnb_ref[...] + bonus * v) * g
    m = mab_ref[...] + _sigmoid(pgc_ref[...].astype(F32) + mbc_ref[...]) * y_c
    o_ref[...] = m.astype(o_ref.dtype)


def _rwkv(p3, hv3, vf3, mab3, mu, w0, a0, kkp, kap, rkp, lng, lnb, mbc, v0, w2p, a2p, g2, v2p,
          col_r, col_k, col_v, col_wa, col_xg, col_gc, ts, hg):
    B, S, _ = p3.shape
    D = w0.shape[-1]
    cw = hg * HEAD
    layer0 = vf3 is None

    def colblk(c0, w, grouped):
        return pl.BlockSpec((None, ts, w), lambda b, gi, t: (b, t, c0 // w + (gi if grouped else 0)))

    def rowpar(c0, w, grouped):
        return pl.BlockSpec((1, w), lambda b, gi, t: (0, c0 // w + (gi if grouped else 0)))

    def matpar(rows):
        return pl.BlockSpec((rows, cw), lambda b, gi, t: (0, gi))

    act = pl.BlockSpec((None, ts, cw), lambda b, gi, t: (b, t, gi))
    in_specs = [colblk(col_r, cw, True), colblk(col_k, cw, True), colblk(col_v, cw, True),
                colblk(col_wa, LANES, False), colblk(col_xg, LANES, False), colblk(col_gc, cw, True), act]
    args = [p3, p3, p3, p3, p3, p3, mab3]
    if not layer0:
        in_specs += [pl.BlockSpec((None, ts, LANES), lambda b, gi, t: (b, t, 0)), act]
        args += [hv3, vf3]
    in_specs += [rowpar(0, cw, True), rowpar(D, cw, True), rowpar(2 * D, cw, True),
                 rowpar(3 * D, LANES, False), rowpar(3 * D + LANES, LANES, False)]
    args += [mu] * 5
    in_specs += [rowpar(0, cw, True)] * 8
    args += [w0, a0, kkp, kap, rkp, lng, lnb, mbc]
    if not layer0:
        in_specs.append(rowpar(0, cw, True))
        args.append(v0)
    in_specs += [matpar(LANES), matpar(LANES), matpar(LANES)]
    args += [w2p, a2p, g2]
    if not layer0:
        in_specs.append(matpar(LANES))
        args.append(v2p)
    out_shape = [jax.ShapeDtypeStruct((B, S, D), BF16)]
    out_specs = [act]
    if layer0:
        out_shape.append(jax.ShapeDtypeStruct((B, S, D), F32))
        out_specs.append(act)
    res = pl.pallas_call(
        functools.partial(_rwkv_kernel, ts=ts, hg=hg, layer0=layer0),
        grid=(B, D // cw, S // ts),
        in_specs=in_specs, out_specs=out_specs, out_shape=out_shape,
        scratch_shapes=[pltpu.VMEM((SUBLANES, cw), F32)] * 3
        + [pltpu.VMEM((SUBLANES, LANES), F32)] * 2
        + [pltpu.VMEM((hg // 2, 2 * HEAD, 2 * HEAD), F32)],
        compiler_params=pltpu.CompilerParams(
            dimension_semantics=("parallel", "parallel", "arbitrary"), vmem_limit_bytes=VMEM_LIMIT),
        name="rwkv",
    )(*args)
    return (res[0], res[1]) if layer0 else (res[0], vf3)


def _mlp_kernel(x_ref, m_ref, wo_ref, g2_ref, w1_ref, w2_ref, fg_ref, o_ref, acc, h2_scr, *, final):
    j = pl.program_id(1)

    @pl.when(j == 0)
    def _():
        xn = x_ref[...] + _dot(m_ref[...], wo_ref[...])
        acc[...] = xn
        h2_scr[...] = _rms(xn, g2_ref[...]).astype(BF16)

    hid = jnp.square(jnp.maximum(_dot(h2_scr[...], w1_ref[...]), 0.0))
    acc[...] += _dot(hid.astype(BF16), w2_ref[...])

    @pl.when(j == pl.num_programs(1) - 1)
    def _():
        o_ref[...] = _rms(acc[...], fg_ref[...]) if final else acc[...]


def _mlp(x2, m2, wo_bf, g2, w1_bf, w2_bf, layer, fg, tm, tf, final):
    T, D = x2.shape
    FF = w1_bf.shape[-1]
    return pl.pallas_call(
        functools.partial(_mlp_kernel, final=final),
        grid=(T // tm, FF // tf),
        in_specs=[pl.BlockSpec((tm, D), lambda i, j: (i, 0)),
                  pl.BlockSpec((tm, D), lambda i, j: (i, 0)),
                  pl.BlockSpec((None, D, D), lambda i, j: (layer, 0, 0)),
                  pl.BlockSpec((1, D), lambda i, j: (0, 0)),
                  pl.BlockSpec((None, D, tf), lambda i, j: (layer, 0, j)),
                  pl.BlockSpec((None, tf, D), lambda i, j: (layer, j, 0)),
                  pl.BlockSpec((1, D), lambda i, j: (0, 0))],
        out_specs=pl.BlockSpec((tm, D), lambda i, j: (i, 0)),
        out_shape=jax.ShapeDtypeStruct((T, D), F32),
        scratch_shapes=[pltpu.VMEM((tm, D), F32), pltpu.VMEM((tm, D), BF16)],
        compiler_params=pltpu.CompilerParams(
            dimension_semantics=("parallel", "arbitrary"), vmem_limit_bytes=VMEM_LIMIT),
        name="mlp",
    )(x2, m2, wo_bf, g2, w1_bf, w2_bf, fg)


def _tile(n, pref):
    t = min(n, pref)
    while n % t:
        t //= 2
    return t


def _col_tile(n, pref):
    best = LANES
    for m in range(1, n // LANES + 1):
        if (n // LANES) % m == 0 and m * LANES <= pref:
            best = m * LANES
    return best


def kernel(x, norm1_g, w_in, merge_b, conv_a_w, lru_conv_w, lru_conv_b, lru_wa, lru_ba, lru_wi, lru_bi,
           lru_a_param, rwkv_mu, rwkv_w0, rwkv_w2, rwkv_a0, rwkv_a2, rwkv_g2, rwkv_kk, rwkv_ka, rwkv_rk,
           rwkv_lnx_g, rwkv_lnx_b, rwkv_v0, rwkv_v1, rwkv_v2, w_out, norm2_g, mlp_w1, mlp_w2, final_g):
    B, S, D = x.shape
    depth = w_in.shape[0]
    T = B * S
    n_in = w_in.shape[-1]
    col_gate = 3 * D + 2 * D
    col_c = col_gate + 3 * D
    col_wa = col_c + 3 * D
    col_xg = col_wa + R_W + R_A
    assert n_in == col_xg + R_G and R_W + R_A == LANES and R_G == LANES and D % LANES == 0

    tm = _tile(T, 2048)
    tn = _col_tile(n_in, 1280)
    ts_ab = _tile(S, 256)
    ts_c = _tile(S, 256)
    hg = 16
    tm_mlp = _tile(T, 1024)
    tf = _tile(mlp_w1.shape[-1], 1024)

    row = lambda vec: vec.reshape(1, -1)
    x2 = x.reshape(T, D)
    vf3 = None
    w_in_bf, w_out_bf = w_in.astype(BF16), w_out.astype(BF16)
    w1_bf, w2_bf = mlp_w1.astype(BF16), mlp_w2.astype(BF16)
    for l in range(depth):
        v1p = v2p = None
        if l > 0:
            v1p = jnp.pad(rwkv_v1[l - 1], ((0, 0), (0, LANES - R_V))).astype(BF16)
            v2p = jnp.pad(rwkv_v2[l - 1], ((0, LANES - R_V), (0, 0))).astype(BF16)
        w2p = jnp.pad(rwkv_w2[l], ((0, R_A), (0, 0))).astype(BF16)
        a2p = jnp.pad(rwkv_a2[l], ((R_W, 0), (0, 0))).astype(BF16)
        g2 = rwkv_g2[l].astype(BF16)
        bw = D // LRU_HEADS
        wa = lru_wa[l].reshape(LRU_HEADS // 2, 2, bw, bw)
        wi = lru_wi[l].reshape(LRU_HEADS // 2, 2, bw, bw)
        z = jnp.zeros_like(wa[:, 0])
        blockdiag = lambda w: jnp.concatenate(
            [jnp.concatenate([w[:, 0], z], axis=-1), jnp.concatenate([z, w[:, 1]], axis=-1)], axis=-2)
        wg = jnp.concatenate([blockdiag(wa), blockdiag(wi)], axis=-1).astype(BF16)
        mb = merge_b[l]

        p2, hv2 = _inproj(x2, row(norm1_g[l]), w_in_bf, l, v1p, tm, tn)
        p3 = p2.reshape(B, S, n_in)
        mab3 = _mix_ab(p3, conv_a_w[l], lru_conv_w[l], row(lru_conv_b[l]), wg, row(lru_ba[l]), row(lru_bi[l]),
                       row(lru_a_param[l]), row(mb[:D]), row(mb[D:2 * D]), ts_ab)
        m3, vf3 = _rwkv(p3, None if l == 0 else hv2.reshape(B, S, LANES), vf3, mab3,
                        row(rwkv_mu[l]), row(rwkv_w0[l]), row(rwkv_a0[l]), row(rwkv_kk[l]), row(rwkv_ka[l]),
                        row(rwkv_rk[l]), row(rwkv_lnx_g[l]), row(rwkv_lnx_b[l]), row(mb[2 * D:]),
                        None if l == 0 else row(rwkv_v0[l - 1]), w2p, a2p, g2, v2p,
                        col_c, col_c + D, col_c + 2 * D, col_wa, col_xg, col_gate + 2 * D, ts_c, hg)
        x2 = _mlp(x2, m3.reshape(T, D), w_out_bf, row(norm2_g[l]), w1_bf, w2_bf, l, row(final_g),
                  tm_mlp, tf, final=(l == depth - 1))
    return x2.reshape(B, S, D)
```

```python
import functools

import jax
import jax.numpy as jnp
from jax import lax
from jax.experimental import pallas as pl
from jax.experimental.pallas import tpu as pltpu

F32 = jnp.float32
BF16 = jnp.bfloat16

EPS = 1e-6
LRU_C = 8.0
LRU_HEADS = 16
HEAD = 64
CHUNK = 64
LNX_EPS = HEAD * 1e-5
EXP_M05 = 0.6065306597126334
LOG2E = 1.4426950408889634
R_W, R_A, R_G, R_V = 64, 64, 128, 32
LANES = 128
SUBLANES = 8
VMEM_LIMIT = 48 * 1024 * 1024


def _dot(a, b):
    return jnp.dot(a, b, preferred_element_type=F32)


def _sigmoid(x):
    return 1.0 / (1.0 + jnp.exp(-x))


def _softplus(x):
    return jnp.maximum(x, 0.0) + jnp.log1p(jnp.exp(-jnp.abs(x)))


def _rms(x, g):
    return x * lax.rsqrt(jnp.mean(x * x, axis=-1, keepdims=True) + EPS) * g


def _head_sums(x):
    lo = lax.broadcasted_iota(jnp.int32, (x.shape[0], LANES), 1) < HEAD
    cols = []
    for i in range(0, x.shape[1], LANES):
        xi = x[:, i:i + LANES]
        s_lo = jnp.sum(jnp.where(lo, xi, 0.0), axis=-1, keepdims=True)
        s_hi = jnp.sum(xi, axis=-1, keepdims=True) - s_lo
        cols.append(jnp.where(lo, s_lo, s_hi))
    return cols[0] if len(cols) == 1 else jnp.concatenate(cols, axis=1)


def _sum_left(ones_bf, x):
    p1 = x.astype(BF16)
    p2 = (x - p1.astype(F32)).astype(BF16)
    return _dot(ones_bf, p1) + _dot(ones_bf, p2)


def _inproj_kernel(*refs, has_v1):
    if has_v1:
        x_ref, g_ref, w_ref, v1_ref, p_ref, hv_ref, h_scr = refs
    else:
        x_ref, g_ref, w_ref, p_ref, h_scr = refs

    @pl.when(pl.program_id(1) == 0)
    def _():
        hb = _rms(x_ref[...], g_ref[...]).astype(BF16)
        h_scr[...] = hb
        if has_v1:
            hv_ref[...] = _dot(hb, v1_ref[...])

    p_ref[...] = _dot(h_scr[...], w_ref[...]).astype(p_ref.dtype)


def _inproj(x2, g, w_bf, layer, v1p_bf, tm, tn):
    T, D = x2.shape
    N = w_bf.shape[-1]
    has_v1 = v1p_bf is not None
    in_specs = [pl.BlockSpec((tm, D), lambda i, j: (i, 0)),
                pl.BlockSpec((1, D), lambda i, j: (0, 0)),
                pl.BlockSpec((None, D, tn), lambda i, j: (layer, 0, j))]
    args = [x2, g, w_bf]
    out_shape = [jax.ShapeDtypeStruct((T, N), BF16)]
    out_specs = [pl.BlockSpec((tm, tn), lambda i, j: (i, j))]
    if has_v1:
        in_specs.append(pl.BlockSpec((D, LANES), lambda i, j: (0, 0)))
        args.append(v1p_bf)
        out_shape.append(jax.ShapeDtypeStruct((T, LANES), F32))
        out_specs.append(pl.BlockSpec((tm, LANES), lambda i, j: (i, 0)))
    res = pl.pallas_call(
        functools.partial(_inproj_kernel, has_v1=has_v1),
        grid=(T // tm, N // tn),
        in_specs=in_specs, out_specs=out_specs, out_shape=out_shape,
        scratch_shapes=[pltpu.VMEM((tm, D), BF16)],
        compiler_params=pltpu.CompilerParams(
            dimension_semantics=("parallel", "arbitrary"), vmem_limit_bytes=VMEM_LIMIT),
        name="inproj",
    )(*args)
    return (res[0], res[1]) if has_v1 else (res[0], None)


def _shifted(tail_ref, x, j, ts):
    w = x.shape[-1]
    x3 = x.reshape(ts // SUBLANES, SUBLANES, w)
    tail3 = tail_ref[...].reshape(1, SUBLANES, w)
    sub = lax.broadcasted_iota(jnp.int32, (1, SUBLANES, w), 1)
    out = []
    for d in range(1, j + 1):
        rot = pltpu.roll(x3, d, 1)
        prev_group = jnp.concatenate([pltpu.roll(tail3, d, 1), rot[:-1]], axis=0)
        out.append(jnp.where(sub < d, prev_group, rot).reshape(ts, w))
    tail_ref[...] = x[ts - SUBLANES:, :]
    return out


N_MIX = 7
MIX_SLICES = 2


def _mix_slice(pq, first_tile, cwa, cwb, cbb, wg, bia, bii, ap, mba, mbb,
               ztail, xbtail, hcar, hs_scr, as_scr, ent_scr, ts):
    ba, ca, xa, xb, gb, ga, gbb = [pq[:, i * LANES:(i + 1) * LANES].astype(F32) for i in range(N_MIX)]

    z = ca * xa
    z1, z2 = _shifted(ztail, z, 2, ts)
    y_a = ba * (z * cwa[2:3] + z1 * cwa[1:2] + z2 * cwa[0:1])

    x1, x2, x3 = _shifted(xbtail, xb, 3, ts)
    u = xb * cwb[3:4] + x1 * cwb[2:3] + x2 * cwb[1:2] + x3 * cwb[0:1] + cbb

    res = _dot(u.astype(BF16), wg)
    gate_a = _sigmoid(res[:, :LANES] + bia)
    gate_i = _sigmoid(res[:, LANES:] + bii)
    log_a = (-LRU_C) * gate_a * _softplus(ap)
    a = jnp.exp(log_a)
    m2 = -jnp.tanh(log_a) * (1.0 + a * a)
    mult = jnp.where(m2 > 0.0, m2 * lax.rsqrt(m2), 0.0)
    row = lax.broadcasted_iota(jnp.int32, (ts, LANES), 0)
    mult = jnp.where((row == 0) & first_tile, 1.0, mult)
    h = u * gate_i * mult

    ng = ts // SUBLANES
    hs_scr[...] = h
    as_scr[...] = a
    cum_a, loc_h = [], []
    for s in range(SUBLANES):
        a_s = as_scr[pl.ds(s, ng, stride=SUBLANES), :]
        u_s = hs_scr[pl.ds(s, ng, stride=SUBLANES), :]
        cum_a.append(a_s * cum_a[-1] if s else a_s)
        loc_h.append(a_s * loc_h[-1] + u_s if s else u_s)
    carry = hcar[...]
    for gi in range(ng):
        ent_scr[gi:gi + 1, :] = carry
        carry = loc_h[-1][gi:gi + 1] + cum_a[-1][gi:gi + 1] * carry
    hcar[...] = carry
    enter = ent_scr[...]
    for s in range(SUBLANES):
        hs_scr[pl.ds(s, ng, stride=SUBLANES), :] = loc_h[s] + cum_a[s] * enter
    h = hs_scr[...]

    gelu = 0.5 * gb * (1.0 + jnp.tanh(0.7978845608028654 * (gb + 0.044715 * gb * gb * gb)))
    return _sigmoid(ga + mba) * y_a + _sigmoid(gbb + mbb) * (h * gelu)


def _inproj_mix_kernel(x_ref, g_ref, w_ref, cwa_ref, cwb_ref, cbb_ref, wg_ref, bia_ref, bii_ref, ap_ref,
                       mba_ref, mbb_ref, o_ref, h_scr, pq_scr, ztail, xbtail, hcar, hs_scr, as_scr, ent_scr,
                       *, ts, nt):
    t = pl.program_id(1)
    q = pl.program_id(2)
    wq = N_MIX * LANES

    @pl.when((q == 0) & (t < nt))
    def _():
        h_scr[...] = _rms(x_ref[...], g_ref[...]).astype(BF16)

    @pl.when(t == 1)
    def _():
        ztail[q] = jnp.zeros(ztail.shape[1:], F32)
        xbtail[q] = jnp.zeros(xbtail.shape[1:], F32)
        hcar[q] = jnp.zeros(hcar.shape[1:], F32)

    def mix():
        pq = pq_scr[q]
        for i in range(MIX_SLICES):
            lanes = slice(i * LANES, (i + 1) * LANES)
            o_ref[:, lanes] = _mix_slice(
                pq[:, i * wq:(i + 1) * wq], t == 1, cwa_ref[:, lanes], cwb_ref[:, lanes], cbb_ref[:, lanes],
                wg_ref[i], bia_ref[:, lanes], bii_ref[:, lanes], ap_ref[:, lanes], mba_ref[:, lanes],
                mbb_ref[:, lanes], ztail.at[q, i], xbtail.at[q, i], hcar.at[q, i],
                hs_scr.at[i], as_scr.at[i], ent_scr.at[i], ts)

    def project():
        pq_scr[q] = _dot(h_scr[...], w_ref[...]).astype(BF16)

    @pl.when(t == 0)
    def _():
        o_ref[...] = jnp.zeros_like(o_ref)
        project()

    @pl.when((t > 0) & (t < nt))
    def _():
        mix()
        project()

    @pl.when(t == nt)
    def _():
        mix()


def _inproj_mix(x3, g, wmix_bf, layer, cwa, cwb, cbb, wg, bia, bii, ap, mba, mbb, ts):
    B, S, D = x3.shape
    gw = MIX_SLICES * LANES
    nq = D // gw
    nt = S // ts
    wq = MIX_SLICES * N_MIX * LANES

    def chan(rows):
        return pl.BlockSpec((rows, gw), lambda b, t, q: (0, q))

    sl = MIX_SLICES
    return pl.pallas_call(
        functools.partial(_inproj_mix_kernel, ts=ts, nt=nt),
        grid=(B, nt + 1, nq),
        in_specs=[pl.BlockSpec((None, ts, D), lambda b, t, q: (b, jnp.minimum(t, nt - 1), 0)),
                  pl.BlockSpec((1, D), lambda b, t, q: (0, 0)),
                  pl.BlockSpec((None, None, D, wq), lambda b, t, q: (layer, q, 0, 0)),
                  chan(cwa.shape[0]), chan(cwb.shape[0]), chan(1),
                  pl.BlockSpec((sl, LANES, 2 * LANES), lambda b, t, q: (q, 0, 0)),
                  chan(1), chan(1), chan(1), chan(1), chan(1)],
        out_specs=pl.BlockSpec((None, ts, gw),
                               lambda b, t, q: (b, jnp.maximum(t - 1, 0), jnp.where(t == 0, 0, q))),
        out_shape=jax.ShapeDtypeStruct((B, S, D), F32),
        scratch_shapes=[pltpu.VMEM((ts, D), BF16), pltpu.VMEM((nq, ts, wq), BF16),
                        pltpu.VMEM((nq, sl, SUBLANES, LANES), F32), pltpu.VMEM((nq, sl, SUBLANES, LANES), F32),
                        pltpu.VMEM((nq, sl, 1, LANES), F32),
                        pltpu.VMEM((sl, ts, LANES), F32), pltpu.VMEM((sl, ts, LANES), F32),
                        pltpu.VMEM((sl, ts // SUBLANES, LANES), F32)],
        compiler_params=pltpu.CompilerParams(
            dimension_semantics=("parallel", "arbitrary", "arbitrary"), vmem_limit_bytes=VMEM_LIMIT),
        name="inproj_mix",
    )(x3, g, wmix_bf, cwa, cwb, cbb, wg, bia, bii, ap, mba, mbb)


def _bmm(a, b):
    return jnp.einsum('nik,nkj->nij', a.astype(BF16), b.astype(BF16), preferred_element_type=F32)


def _bmm_nt(a, b):
    return jnp.einsum('nik,njk->nij', a.astype(BF16), b.astype(BF16), preferred_element_type=F32)


def _bmm_tn(a, b):
    return jnp.einsum('nti,ntj->nij', a.astype(BF16), b.astype(BF16), preferred_element_type=F32)


def _bd(x, bdmask):
    xb = x.astype(BF16)
    return jnp.where(bdmask, jnp.concatenate([xb, xb], axis=1), jnp.zeros((), BF16))


def _tri_inv(A, row, hcol, bdmask):
    def blk(s):
        return (row // s) == (hcol // s)

    def pmm(x, y):
        return _bmm(x, _bd(y, bdmask))

    L = A.shape[1]
    eye = (row == hcol).astype(F32)
    Ad = jnp.where(blk(SUBLANES), A, 0.0)
    T = eye + Ad
    P = Ad
    Pbd = _bd(P, bdmask)
    s = 2
    while s < SUBLANES:
        P = _bmm(P, Pbd)
        Pbd = _bd(P, bdmask)
        T = T + _bmm(T, Pbd)
        s *= 2
    s = SUBLANES
    while s < L:
        Aoff = jnp.where(blk(2 * s) & jnp.logical_not(blk(s)), A, 0.0)
        slabs = [T[:, i * s:(i + 1) * s] for i in range(L // s)]
        low = jnp.concatenate(slabs[1::2], axis=1)
        low = low + pmm(pmm(low, Aoff), T)
        for idx, i in enumerate(range(1, L // s, 2)):
            slabs[i] = low[:, idx * s:(idx + 1) * s]
        T = jnp.concatenate(slabs, axis=1)
        s *= 2
    return T


def _rwkv_kernel(*refs, ts, hg, layer0):
    it = iter(refs)
    pr_ref, pk_ref, pv_ref, pwa_ref, pxg_ref, pgc_ref, mab_ref = [next(it) for _ in range(7)]
    if not layer0:
        hv_ref, vf_ref = next(it), next(it)
    (mur_ref, muk_ref, muv_ref, muwa_ref, mug_ref, w0_ref, a0_ref, kk_ref, ka_ref, rk_ref,
     lng_ref, lnb_ref, mbc_ref) = [next(it) for _ in range(13)]
    if not layer0:
        v0_ref = next(it)
    w2_ref, a2_ref, g2_ref = next(it), next(it), next(it)
    if not layer0:
        v2_ref = next(it)
    o_ref = next(it)
    if layer0:
        vfo_ref = next(it)
    rpad, kpad, vpad, wapad, xgpad, st_scr = [next(it) for _ in range(6)]

    t = pl.program_id(2)
    cw = hg * HEAD
    nc = ts // CHUNK
    npair = hg // 2
    L = CHUNK

    @pl.when(t == 0)
    def _():
        for pad in (rpad, kpad, vpad, wapad, xgpad):
            pad[...] = jnp.zeros_like(pad)
        st_scr[...] = jnp.zeros_like(st_scr)

    ri = lax.broadcasted_iota(jnp.int32, (ts, ts), 0)
    ci = lax.broadcasted_iota(jnp.int32, (ts, ts), 1)

    def token_shift(pad, x_ref, mu_ref):
        x = x_ref[...].astype(F32)
        (prev,) = _shifted(pad, x, 1, ts)
        return x + (prev - x) * mu_ref[...]

    r = token_shift(rpad, pr_ref, mur_ref)
    k = token_shift(kpad, pk_ref, muk_ref)
    v = token_shift(vpad, pv_ref, muv_ref)
    xwa = token_shift(wapad, pwa_ref, muwa_ref)
    xg = token_shift(xgpad, pxg_ref, mug_ref)

    wl = w0_ref[...] + _dot(jnp.tanh(xwa).astype(BF16), w2_ref[...])
    ld = (-LOG2E * EXP_M05) * _sigmoid(wl)
    a = _sigmoid(a0_ref[...] + _dot(xwa.astype(BF16), a2_ref[...]))
    g = _dot(_sigmoid(xg).astype(BF16), g2_ref[...])
    if layer0:
        vfo_ref[...] = v
    else:
        mix = _sigmoid(v0_ref[...] + _dot(hv_ref[...].astype(BF16), v2_ref[...]))
        v = v + (vf_ref[...] - v) * mix

    ri = lax.broadcasted_iota(jnp.int32, (ts, ts), 0)
    ci = lax.broadcasted_iota(jnp.int32, (ts, ts), 1)
    tri = jnp.where(((ri // L) == (ci // L)) & (ci <= ri), 1.0, 0.0).astype(BF16)

    c = _sum_left(tri, ld)
    e_c = jnp.exp2(c)
    e_n = 1.0 / e_c
    e_cl = jnp.broadcast_to(e_c.reshape(nc, L, cw)[:, L - 1:L, :], (nc, L, cw)).reshape(ts, cw)

    kk = k * kk_ref[...]
    kk = kk * lax.rsqrt(jnp.maximum(_head_sums(kk * kk), 1e-24))
    k = k * (1.0 + (a - 1.0) * ka_ref[...])
    bt = kk * a * e_n
    kt = k * e_n
    Rt_f = (r * e_c).astype(BF16)
    At_f = (-kk * jnp.exp2(c - ld)).astype(BF16)
    Bt_f = bt.astype(BF16)
    Kt_f = kt.astype(BF16)
    Bh_f = (bt * e_cl).astype(BF16)
    Kh_f = (kt * e_cl).astype(BF16)
    v_b = v.astype(BF16)

    PW = 2 * HEAD
    prow = lax.broadcasted_iota(jnp.int32, (L, PW), 0)
    hcol = lax.broadcasted_iota(jnp.int32, (L, PW), 1) % HEAD
    low_incl = hcol <= prow
    low_strict = hcol < prow
    brow = lax.broadcasted_iota(jnp.int32, (2 * L, PW), 0)
    bcol = lax.broadcasted_iota(jnp.int32, (2 * L, PW), 1)
    bdmask = (brow // L) == (bcol // HEAD)
    eye2 = brow == bcol

    def pairs(x):
        per_pair = [x[:, p * PW:(p + 1) * PW].reshape(nc, 1, L, PW) for p in range(npair)]
        return jnp.concatenate(per_pair, axis=1).reshape(nc * npair, L, PW)

    Rt, At, Bt, Kt, Bh, Kh, vp = [pairs(x) for x in (Rt_f, At_f, Bt_f, Kt_f, Bh_f, Kh_f, v_b)]
    zero_p = jnp.zeros_like(vp)
    AA = _bmm_nt(jnp.concatenate([Rt, At], axis=1),
                 jnp.concatenate([_bd(Bt, bdmask), _bd(Kt, bdmask)], axis=1))
    Arb = jnp.where(low_incl, AA[:, :L, :PW], 0.0)
    Ark = jnp.where(low_incl, AA[:, :L, PW:], 0.0)
    Aab = jnp.where(low_strict, AA[:, L:, :PW], 0.0)
    Aak = jnp.where(low_strict, AA[:, L:, PW:], 0.0)
    T = _tri_inv(Aab, prow, hcol, bdmask)
    AakV = _bmm(Aak, _bd(vp, bdmask))
    WU = _bmm(T, jnp.concatenate([_bd(At, bdmask), _bd(AakV, bdmask)], axis=2))
    W, U0 = WU[:, :, :PW], WU[:, :, PW:]
    QY = _bmm(jnp.concatenate([Arb, Ark], axis=2),
              jnp.concatenate([jnp.concatenate([_bd(W, bdmask), _bd(U0, bdmask)], axis=2),
                               jnp.concatenate([jnp.zeros((nc * npair, 2 * L, PW), BF16), _bd(vp, bdmask)],
                                               axis=2)], axis=1))
    Q = Rt.astype(F32) + QY[:, :, :PW]
    Y0 = QY[:, :, PW:]
    MG = _bmm_tn(jnp.concatenate([Bh, Kh], axis=1),
                 jnp.concatenate([WU.astype(BF16), jnp.concatenate([zero_p, vp], axis=2)], axis=1))
    pl_row = pairs(e_c)[:, L - 1:L, :]
    M = jnp.where(bdmask, MG[:, :, :PW], 0.0) + jnp.where(eye2, pl_row, 0.0)
    G = jnp.where(bdmask, MG[:, :, PW:], 0.0)
    QM = jnp.concatenate([Q, M], axis=1).astype(BF16)

    H = st_scr[...]
    ys = []
    for ch in range(nc):
        sl = slice(ch * npair, (ch + 1) * npair)
        YH = _bmm(QM[sl], H)
        ys.append(YH[:, :L, :] + Y0[sl])
        H = YH[:, L:, :] + G[sl]
    st_scr[...] = H
    y = jnp.concatenate(
        [jnp.concatenate([ys[ch][p] for ch in range(nc)], axis=0) for p in range(npair)], axis=-1)
    inv_n = 1.0 / HEAD
    yc = y - _head_sums(y) * inv_n
    var = _head_sums(yc * yc) * inv_n
    yn = yc * lax.rsqrt(var + LNX_EPS)
    bonus = _head_sums(r * k * rk_ref[...])
    y_c = (yn * lng_ref[...] + lnb_ref[...] + bonus * v) * g
    m = mab_ref[...] + _sigmoid(pgc_ref[...].astype(F32) + mbc_ref[...]) * y_c
    o_ref[...] = m.astype(o_ref.dtype)


def _rwkv(p3, hv3, vf3, mab3, mu, w0, a0, kkp, kap, rkp, lng, lnb, mbc, v0, w2p, a2p, g2, v2p,
          col_r, col_k, col_v, col_wa, col_xg, col_gc, ts, hg):
    B, S, _ = p3.shape
    D = w0.shape[-1]
    cw = hg * HEAD
    layer0 = vf3 is None

    def colblk(c0, w, grouped):
        return pl.BlockSpec((None, ts, w), lambda b, gi, t: (b, t, c0 // w + (gi if grouped else 0)))

    def rowpar(c0, w, grouped):
        return pl.BlockSpec((1, w), lambda b, gi, t: (0, c0 // w + (gi if grouped else 0)))

    def matpar(rows):
        return pl.BlockSpec((rows, cw), lambda b, gi, t: (0, gi))

    act = pl.BlockSpec((None, ts, cw), lambda b, gi, t: (b, t, gi))
    in_specs = [colblk(col_r, cw, True), colblk(col_k, cw, True), colblk(col_v, cw, True),
                colblk(col_wa, LANES, False), colblk(col_xg, LANES, False), colblk(col_gc, cw, True), act]
    args = [p3, p3, p3, p3, p3, p3, mab3]
    if not layer0:
        in_specs += [pl.BlockSpec((None, ts, LANES), lambda b, gi, t: (b, t, 0)), act]
        args += [hv3, vf3]
    in_specs += [rowpar(0, cw, True), rowpar(D, cw, True), rowpar(2 * D, cw, True),
                 rowpar(3 * D, LANES, False), rowpar(3 * D + LANES, LANES, False)]
    args += [mu] * 5
    in_specs += [rowpar(0, cw, True)] * 8
    args += [w0, a0, kkp, kap, rkp, lng, lnb, mbc]
    if not layer0:
        in_specs.append(rowpar(0, cw, True))
        args.append(v0)
    in_specs += [matpar(LANES), matpar(LANES), matpar(LANES)]
    args += [w2p, a2p, g2]
    if not layer0:
        in_specs.append(matpar(LANES))
        args.append(v2p)
    out_shape = [jax.ShapeDtypeStruct((B, S, D), BF16)]
    out_specs = [act]
    if layer0:
        out_shape.append(jax.ShapeDtypeStruct((B, S, D), F32))
        out_specs.append(act)
    res = pl.pallas_call(
        functools.partial(_rwkv_kernel, ts=ts, hg=hg, layer0=layer0),
        grid=(B, D // cw, S // ts),
        in_specs=in_specs, out_specs=out_specs, out_shape=out_shape,
        scratch_shapes=[pltpu.VMEM((SUBLANES, cw), F32)] * 3
        + [pltpu.VMEM((SUBLANES, LANES), F32)] * 2
        + [pltpu.VMEM((hg // 2, 2 * HEAD, 2 * HEAD), F32)],
        compiler_params=pltpu.CompilerParams(
            dimension_semantics=("parallel", "parallel", "arbitrary"), vmem_limit_bytes=VMEM_LIMIT),
        name="rwkv",
    )(*args)
    return (res[0], res[1]) if layer0 else (res[0], vf3)


def _mlp_kernel(x_ref, m_ref, wo_ref, g2_ref, w1_ref, w2_ref, fg_ref, o_ref, acc, h2_scr, *, final):
    j = pl.program_id(1)

    @pl.when(j == 0)
    def _():
        xn = x_ref[...] + _dot(m_ref[...], wo_ref[...])
        acc[...] = xn
        h2_scr[...] = _rms(xn, g2_ref[...]).astype(BF16)

    hid = jnp.square(jnp.maximum(_dot(h2_scr[...], w1_ref[...]), 0.0))
    acc[...] += _dot(hid.astype(BF16), w2_ref[...])

    @pl.when(j == pl.num_programs(1) - 1)
    def _():
        o_ref[...] = _rms(acc[...], fg_ref[...]) if final else acc[...]


def _mlp(x2, m2, wo_bf, g2, w1_bf, w2_bf, layer, fg, tm, tf, final):
    T, D = x2.shape
    FF = w1_bf.shape[-1]
    return pl.pallas_call(
        functools.partial(_mlp_kernel, final=final),
        grid=(T // tm, FF // tf),
        in_specs=[pl.BlockSpec((tm, D), lambda i, j: (i, 0)),
                  pl.BlockSpec((tm, D), lambda i, j: (i, 0)),
                  pl.BlockSpec((None, D, D), lambda i, j: (layer, 0, 0)),
                  pl.BlockSpec((1, D), lambda i, j: (0, 0)),
                  pl.BlockSpec((None, D, tf), lambda i, j: (layer, 0, j)),
                  pl.BlockSpec((None, tf, D), lambda i, j: (layer, j, 0)),
                  pl.BlockSpec((1, D), lambda i, j: (0, 0))],
        out_specs=pl.BlockSpec((tm, D), lambda i, j: (i, 0)),
        out_shape=jax.ShapeDtypeStruct((T, D), F32),
        scratch_shapes=[pltpu.VMEM((tm, D), F32), pltpu.VMEM((tm, D), BF16)],
        compiler_params=pltpu.CompilerParams(
            dimension_semantics=("parallel", "arbitrary"), vmem_limit_bytes=VMEM_LIMIT),
        name="mlp",
    )(x2, m2, wo_bf, g2, w1_bf, w2_bf, fg)


def _tile(n, pref):
    t = min(n, pref)
    while n % t:
        t //= 2
    return t


def kernel(x, norm1_g, w_in, merge_b, conv_a_w, lru_conv_w, lru_conv_b, lru_wa, lru_ba, lru_wi, lru_bi,
           lru_a_param, rwkv_mu, rwkv_w0, rwkv_w2, rwkv_a0, rwkv_a2, rwkv_g2, rwkv_kk, rwkv_ka, rwkv_rk,
           rwkv_lnx_g, rwkv_lnx_b, rwkv_v0, rwkv_v1, rwkv_v2, w_out, norm2_g, mlp_w1, mlp_w2, final_g):
    B, S, D = x.shape
    depth = w_in.shape[0]
    T = B * S
    n_in = w_in.shape[-1]
    assert n_in == 11 * D + R_W + R_A + R_G and R_W + R_A == LANES and R_G == LANES
    assert D % (MIX_SLICES * LANES) == 0

    tm_c = _tile(T, 512)
    ts_ab = _tile(S, 1024)
    ts_c = _tile(S, 256)
    hg = 16
    tm_mlp = _tile(T, 1024)
    tf = _tile(mlp_w1.shape[-1], 1024)

    row = lambda vec: vec.reshape(1, -1)
    x2 = x.reshape(T, D)
    vf3 = None
    w_in_bf, w_out_bf = w_in.astype(BF16), w_out.astype(BF16)
    w1_bf, w2_bf = mlp_w1.astype(BF16), mlp_w2.astype(BF16)
    n_mix = N_MIX * D
    nq = D // (MIX_SLICES * LANES)
    wmix_bf = (w_in_bf[:, :, :n_mix].reshape(depth, D, N_MIX, nq, MIX_SLICES, LANES)
               .transpose(0, 3, 1, 4, 2, 5).reshape(depth, nq, D, MIX_SLICES * N_MIX * LANES))
    wc_bf = w_in_bf[:, :, n_mix:]
    n_c = n_in - n_mix
    for l in range(depth):
        v1p = v2p = None
        if l > 0:
            v1p = jnp.pad(rwkv_v1[l - 1], ((0, 0), (0, LANES - R_V))).astype(BF16)
            v2p = jnp.pad(rwkv_v2[l - 1], ((0, LANES - R_V), (0, 0))).astype(BF16)
        w2p = jnp.pad(rwkv_w2[l], ((0, R_A), (0, 0))).astype(BF16)
        a2p = jnp.pad(rwkv_a2[l], ((R_W, 0), (0, 0))).astype(BF16)
        g2 = rwkv_g2[l].astype(BF16)
        bw = D // LRU_HEADS
        wa = lru_wa[l].reshape(LRU_HEADS // 2, 2, bw, bw)
        wi = lru_wi[l].reshape(LRU_HEADS // 2, 2, bw, bw)
        z = jnp.zeros_like(wa[:, 0])
        blockdiag = lambda w: jnp.concatenate(
            [jnp.concatenate([w[:, 0], z], axis=-1), jnp.concatenate([z, w[:, 1]], axis=-1)], axis=-2)
        wg = jnp.concatenate([blockdiag(wa), blockdiag(wi)], axis=-1).astype(BF16)
        mb = merge_b[l]

        mab3 = _inproj_mix(x2.reshape(B, S, D), row(norm1_g[l]), wmix_bf, l, conv_a_w[l], lru_conv_w[l],
                           row(lru_conv_b[l]), wg, row(lru_ba[l]), row(lru_bi[l]), row(lru_a_param[l]),
                           row(mb[:D]), row(mb[D:2 * D]), ts_ab)
        pc2, hv2 = _inproj(x2, row(norm1_g[l]), wc_bf, l, v1p, tm_c, n_c)
        pc3 = pc2.reshape(B, S, n_c)
        m3, vf3 = _rwkv(pc3, None if l == 0 else hv2.reshape(B, S, LANES), vf3, mab3,
                        row(rwkv_mu[l]), row(rwkv_w0[l]), row(rwkv_a0[l]), row(rwkv_kk[l]), row(rwkv_ka[l]),
                        row(rwkv_rk[l]), row(rwkv_lnx_g[l]), row(rwkv_lnx_b[l]), row(mb[2 * D:]),
                        None if l == 0 else row(rwkv_v0[l - 1]), w2p, a2p, g2, v2p,
                        D, 2 * D, 3 * D, 4 * D, 4 * D + R_W + R_A, 0, ts_c, hg)
        x2 = _mlp(x2, m3.reshape(T, D), w_out_bf, row(norm2_g[l]), w1_bf, w2_bf, l, row(final_g),
                  tm_mlp, tf, final=(l == depth - 1))
    return x2.reshape(B, S, D)
```

```python
import functools

import jax
import jax.numpy as jnp
from jax import lax
from jax.experimental import pallas as pl
from jax.experimental.pallas import tpu as pltpu

F32 = jnp.float32
BF16 = jnp.bfloat16

EPS = 1e-6
LRU_C = 8.0
LRU_HEADS = 16
HEAD = 64
CHUNK = 64
LNX_EPS = HEAD * 1e-5
EXP_M05 = 0.6065306597126334
LOG2E = 1.4426950408889634
R_W, R_A, R_G, R_V = 64, 64, 128, 32
LANES = 128
SUBLANES = 8
VMEM_LIMIT = 48 * 1024 * 1024


def _dot(a, b):
    return jnp.dot(a, b, preferred_element_type=F32)


def _sigmoid(x):
    return 1.0 / (1.0 + jnp.exp(-x))


def _softplus(x):
    return jnp.maximum(x, 0.0) + jnp.log1p(jnp.exp(-jnp.abs(x)))


def _rms(x, g):
    return x * lax.rsqrt(jnp.mean(x * x, axis=-1, keepdims=True) + EPS) * g


def _head_sums(x):
    lo = lax.broadcasted_iota(jnp.int32, (x.shape[0], LANES), 1) < HEAD
    cols = []
    for i in range(0, x.shape[1], LANES):
        xi = x[:, i:i + LANES]
        s_lo = jnp.sum(jnp.where(lo, xi, 0.0), axis=-1, keepdims=True)
        s_hi = jnp.sum(xi, axis=-1, keepdims=True) - s_lo
        cols.append(jnp.where(lo, s_lo, s_hi))
    return cols[0] if len(cols) == 1 else jnp.concatenate(cols, axis=1)


def _sum_left(ones_bf, x):
    p1 = x.astype(BF16)
    p2 = (x - p1.astype(F32)).astype(BF16)
    return _dot(ones_bf, p1) + _dot(ones_bf, p2)


def _inproj_kernel(*refs, widths, has_v1):
    x_ref, g_ref = refs[:2]
    w_refs = refs[2:2 + len(widths)]
    rest = refs[2 + len(widths):]
    if has_v1:
        v1_ref, p_ref, hv_ref = rest
    else:
        (p_ref,) = rest

    hb = _rms(x_ref[...], g_ref[...]).astype(BF16)
    if has_v1:
        hv_ref[...] = _dot(hb, v1_ref[...])
    off = 0
    for w_ref, width in zip(w_refs, widths):
        p_ref[:, off:off + width] = _dot(hb, w_ref[...]).astype(p_ref.dtype)
        off += width


def _inproj(x2, g, w_bf, layer, pieces, v1p_bf, tm):
    T, D = x2.shape
    widths = tuple(w for _, w in pieces)
    has_v1 = v1p_bf is not None
    in_specs = [pl.BlockSpec((tm, D), lambda i: (i, 0)), pl.BlockSpec((1, D), lambda i: (0, 0))]
    in_specs += [pl.BlockSpec((None, D, w), lambda i, c=c0 // w: (layer, 0, c)) for c0, w in pieces]
    args = [x2, g] + [w_bf] * len(pieces)
    n_out = sum(widths)
    out_shape = [jax.ShapeDtypeStruct((T, n_out), BF16)]
    out_specs = [pl.BlockSpec((tm, n_out), lambda i: (i, 0))]
    if has_v1:
        in_specs.append(pl.BlockSpec((D, LANES), lambda i: (0, 0)))
        args.append(v1p_bf)
        out_shape.append(jax.ShapeDtypeStruct((T, LANES), F32))
        out_specs.append(pl.BlockSpec((tm, LANES), lambda i: (i, 0)))
    res = pl.pallas_call(
        functools.partial(_inproj_kernel, widths=widths, has_v1=has_v1),
        grid=(T // tm,),
        in_specs=in_specs, out_specs=out_specs, out_shape=out_shape,
        compiler_params=pltpu.CompilerParams(
            dimension_semantics=("parallel",), vmem_limit_bytes=VMEM_LIMIT),
        name="inproj",
    )(*args)
    return (res[0], res[1]) if has_v1 else (res[0], None)


def _shifted(tail_ref, x, j, ts):
    w = x.shape[-1]
    x3 = x.reshape(ts // SUBLANES, SUBLANES, w)
    tail3 = tail_ref[...].reshape(1, SUBLANES, w)
    sub = lax.broadcasted_iota(jnp.int32, (1, SUBLANES, w), 1)
    out = []
    for d in range(1, j + 1):
        rot = pltpu.roll(x3, d, 1)
        prev_group = jnp.concatenate([pltpu.roll(tail3, d, 1), rot[:-1]], axis=0)
        out.append(jnp.where(sub < d, prev_group, rot).reshape(ts, w))
    tail_ref[...] = x[ts - SUBLANES:, :]
    return out


N_MIX = 7
MIX_SLICES = 2


def _mix_slice(cols, first_tile, cwa, cwb, cbb, wg, bia, bii, ap, mba, mbb,
               ztail, xbtail, hcar, hs_scr, as_scr, ent_scr, ts):
    ba, ca, xa, xb, gb, ga, gbb = [c.astype(F32) for c in cols]

    z = ca * xa
    z1, z2 = _shifted(ztail, z, 2, ts)
    y_a = ba * (z * cwa[2:3] + z1 * cwa[1:2] + z2 * cwa[0:1])

    x1, x2, x3 = _shifted(xbtail, xb, 3, ts)
    u = xb * cwb[3:4] + x1 * cwb[2:3] + x2 * cwb[1:2] + x3 * cwb[0:1] + cbb

    res = _dot(u.astype(BF16), wg)
    gate_a = _sigmoid(res[:, :LANES] + bia)
    gate_i = _sigmoid(res[:, LANES:] + bii)
    log_a = (-LRU_C) * gate_a * _softplus(ap)
    a = jnp.exp(log_a)
    m2 = -jnp.tanh(log_a) * (1.0 + a * a)
    mult = jnp.where(m2 > 0.0, m2 * lax.rsqrt(m2), 0.0)
    row = lax.broadcasted_iota(jnp.int32, (ts, LANES), 0)
    mult = jnp.where((row == 0) & first_tile, 1.0, mult)
    h = u * gate_i * mult

    ng = ts // SUBLANES
    hs_scr[...] = h
    as_scr[...] = a
    cum_a, loc_h = [], []
    for s in range(SUBLANES):
        a_s = as_scr[pl.ds(s, ng, stride=SUBLANES), :]
        u_s = hs_scr[pl.ds(s, ng, stride=SUBLANES), :]
        cum_a.append(a_s * cum_a[-1] if s else a_s)
        loc_h.append(a_s * loc_h[-1] + u_s if s else u_s)
    carry = hcar[...]
    for gi in range(ng):
        ent_scr[gi:gi + 1, :] = carry
        carry = loc_h[-1][gi:gi + 1] + cum_a[-1][gi:gi + 1] * carry
    hcar[...] = carry
    enter = ent_scr[...]
    for s in range(SUBLANES):
        hs_scr[pl.ds(s, ng, stride=SUBLANES), :] = loc_h[s] + cum_a[s] * enter
    h = hs_scr[...]

    gelu = 0.5 * gb * (1.0 + jnp.tanh(0.7978845608028654 * (gb + 0.044715 * gb * gb * gb)))
    return _sigmoid(ga + mba) * y_a + _sigmoid(gbb + mbb) * (h * gelu)


def _inproj_mix_kernel(*refs, ts, nt):
    x_ref, g_ref = refs[:2]
    w_refs = refs[2:2 + N_MIX]
    (cwa_ref, cwb_ref, cbb_ref, wg_ref, bia_ref, bii_ref, ap_ref, mba_ref, mbb_ref, o_ref,
     h_scr, pq_scr, ztail, xbtail, hcar, hs_scr, as_scr, ent_scr) = refs[2 + N_MIX:]
    t = pl.program_id(1)
    q = pl.program_id(2)
    gw = MIX_SLICES * LANES

    @pl.when((q == 0) & (t < nt))
    def _():
        h_scr[...] = _rms(x_ref[...], g_ref[...]).astype(BF16)

    @pl.when(t == 1)
    def _():
        ztail[q] = jnp.zeros(ztail.shape[1:], F32)
        xbtail[q] = jnp.zeros(xbtail.shape[1:], F32)
        hcar[q] = jnp.zeros(hcar.shape[1:], F32)

    def mix():
        pq = pq_scr[q]
        for i in range(MIX_SLICES):
            lanes = slice(i * LANES, (i + 1) * LANES)
            cols = [pq[:, g * gw + i * LANES:g * gw + (i + 1) * LANES] for g in range(N_MIX)]
            o_ref[:, lanes] = _mix_slice(
                cols, t == 1, cwa_ref[:, lanes], cwb_ref[:, lanes], cbb_ref[:, lanes],
                wg_ref[i], bia_ref[:, lanes], bii_ref[:, lanes], ap_ref[:, lanes], mba_ref[:, lanes],
                mbb_ref[:, lanes], ztail.at[q, i], xbtail.at[q, i], hcar.at[q, i],
                hs_scr.at[i], as_scr.at[i], ent_scr.at[i], ts)

    def project():
        hb = h_scr[...]
        for g in range(N_MIX):
            pq_scr[q, :, g * gw:(g + 1) * gw] = _dot(hb, w_refs[g][...]).astype(BF16)

    @pl.when(t == 0)
    def _():
        o_ref[...] = jnp.zeros_like(o_ref)
        project()

    @pl.when((t > 0) & (t < nt))
    def _():
        mix()
        project()

    @pl.when(t == nt)
    def _():
        mix()


def _inproj_mix(x3, g, w_bf, layer, cwa, cwb, cbb, wg, bia, bii, ap, mba, mbb, ts):
    B, S, D = x3.shape
    gw = MIX_SLICES * LANES
    nq = D // gw
    nt = S // ts
    wq = N_MIX * gw

    def chan(rows):
        return pl.BlockSpec((rows, gw), lambda b, t, q: (0, q))

    sl = MIX_SLICES
    return pl.pallas_call(
        functools.partial(_inproj_mix_kernel, ts=ts, nt=nt),
        grid=(B, nt + 1, nq),
        in_specs=[pl.BlockSpec((None, ts, D), lambda b, t, q: (b, jnp.minimum(t, nt - 1), 0)),
                  pl.BlockSpec((1, D), lambda b, t, q: (0, 0))]
        + [pl.BlockSpec((None, D, gw), lambda b, t, q, grp=grp: (layer, 0, grp * nq + q)) for grp in range(N_MIX)]
        + [chan(cwa.shape[0]), chan(cwb.shape[0]), chan(1),
                  pl.BlockSpec((sl, LANES, 2 * LANES), lambda b, t, q: (q, 0, 0)),
                  chan(1), chan(1), chan(1), chan(1), chan(1)],
        out_specs=pl.BlockSpec((None, ts, gw),
                               lambda b, t, q: (b, jnp.maximum(t - 1, 0), jnp.where(t == 0, 0, q))),
        out_shape=jax.ShapeDtypeStruct((B, S, D), F32),
        scratch_shapes=[pltpu.VMEM((ts, D), BF16), pltpu.VMEM((nq, ts, wq), BF16),
                        pltpu.VMEM((nq, sl, SUBLANES, LANES), F32), pltpu.VMEM((nq, sl, SUBLANES, LANES), F32),
                        pltpu.VMEM((nq, sl, 1, LANES), F32),
                        pltpu.VMEM((sl, ts, LANES), F32), pltpu.VMEM((sl, ts, LANES), F32),
                        pltpu.VMEM((sl, ts // SUBLANES, LANES), F32)],
        compiler_params=pltpu.CompilerParams(
            dimension_semantics=("parallel", "arbitrary", "arbitrary"), vmem_limit_bytes=VMEM_LIMIT),
        name="inproj_mix",
    )(x3, g, *([w_bf] * N_MIX), cwa, cwb, cbb, wg, bia, bii, ap, mba, mbb)


def _bmm(a, b):
    return jnp.einsum('nik,nkj->nij', a.astype(BF16), b.astype(BF16), preferred_element_type=F32)


def _bmm_nt(a, b):
    return jnp.einsum('nik,njk->nij', a.astype(BF16), b.astype(BF16), preferred_element_type=F32)


def _bmm_tn(a, b):
    return jnp.einsum('nti,ntj->nij', a.astype(BF16), b.astype(BF16), preferred_element_type=F32)


def _bd(x, bdmask):
    xb = x.astype(BF16)
    return jnp.where(bdmask, jnp.concatenate([xb, xb], axis=1), jnp.zeros((), BF16))


def _tri_inv(A, row, hcol, bdmask):
    def blk(s):
        return (row // s) == (hcol // s)

    def pmm(x, y):
        return _bmm(x, _bd(y, bdmask))

    L = A.shape[1]
    eye = (row == hcol).astype(F32)
    Ad = jnp.where(blk(SUBLANES), A, 0.0)
    T = eye + Ad
    P = Ad
    Pbd = _bd(P, bdmask)
    s = 2
    while s < SUBLANES:
        P = _bmm(P, Pbd)
        Pbd = _bd(P, bdmask)
        T = T + _bmm(T, Pbd)
        s *= 2
    s = SUBLANES
    while s < L:
        Aoff = jnp.where(blk(2 * s) & jnp.logical_not(blk(s)), A, 0.0)
        slabs = [T[:, i * s:(i + 1) * s] for i in range(L // s)]
        low = jnp.concatenate(slabs[1::2], axis=1)
        low = low + pmm(pmm(low, Aoff), T)
        for idx, i in enumerate(range(1, L // s, 2)):
            slabs[i] = low[:, idx * s:(idx + 1) * s]
        T = jnp.concatenate(slabs, axis=1)
        s *= 2
    return T


def _rwkv_kernel(*refs, ts, hg, layer0):
    it = iter(refs)
    pr_ref, pk_ref, pv_ref, pwa_ref, pxg_ref, pgc_ref, mab_ref = [next(it) for _ in range(7)]
    if not layer0:
        hv_ref, vf_ref = next(it), next(it)
    (mur_ref, muk_ref, muv_ref, muwa_ref, mug_ref, w0_ref, a0_ref, kk_ref, ka_ref, rk_ref,
     lng_ref, lnb_ref, mbc_ref) = [next(it) for _ in range(13)]
    if not layer0:
        v0_ref = next(it)
    w2_ref, a2_ref, g2_ref = next(it), next(it), next(it)
    if not layer0:
        v2_ref = next(it)
    o_ref = next(it)
    if layer0:
        vfo_ref = next(it)
    rpad, kpad, vpad, wapad, xgpad, st_scr = [next(it) for _ in range(6)]

    t = pl.program_id(2)
    cw = hg * HEAD
    nc = ts // CHUNK
    npair = hg // 2
    L = CHUNK

    @pl.when(t == 0)
    def _():
        for pad in (rpad, kpad, vpad, wapad, xgpad):
            pad[...] = jnp.zeros_like(pad)
        st_scr[...] = jnp.zeros_like(st_scr)

    ri = lax.broadcasted_iota(jnp.int32, (ts, ts), 0)
    ci = lax.broadcasted_iota(jnp.int32, (ts, ts), 1)

    def token_shift(pad, x_ref, mu_ref):
        x = x_ref[...].astype(F32)
        (prev,) = _shifted(pad, x, 1, ts)
        return x + (prev - x) * mu_ref[...]

    r = token_shift(rpad, pr_ref, mur_ref)
    k = token_shift(kpad, pk_ref, muk_ref)
    v = token_shift(vpad, pv_ref, muv_ref)
    xwa = token_shift(wapad, pwa_ref, muwa_ref)
    xg = token_shift(xgpad, pxg_ref, mug_ref)

    wl = w0_ref[...] + _dot(jnp.tanh(xwa).astype(BF16), w2_ref[...])
    ld = (-LOG2E * EXP_M05) * _sigmoid(wl)
    a = _sigmoid(a0_ref[...] + _dot(xwa.astype(BF16), a2_ref[...]))
    g = _dot(_sigmoid(xg).astype(BF16), g2_ref[...])
    if layer0:
        vfo_ref[...] = v
    else:
        mix = _sigmoid(v0_ref[...] + _dot(hv_ref[...].astype(BF16), v2_ref[...]))
        v = v + (vf_ref[...] - v) * mix

    ri = lax.broadcasted_iota(jnp.int32, (ts, ts), 0)
    ci = lax.broadcasted_iota(jnp.int32, (ts, ts), 1)
    tri = jnp.where(((ri // L) == (ci // L)) & (ci <= ri), 1.0, 0.0).astype(BF16)

    c = _sum_left(tri, ld)
    e_c = jnp.exp2(c)
    e_n = 1.0 / e_c
    e_cl = jnp.broadcast_to(e_c.reshape(nc, L, cw)[:, L - 1:L, :], (nc, L, cw)).reshape(ts, cw)

    kk = k * kk_ref[...]
    kk = kk * lax.rsqrt(jnp.maximum(_head_sums(kk * kk), 1e-24))
    k = k * (1.0 + (a - 1.0) * ka_ref[...])
    bt = kk * a * e_n
    kt = k * e_n
    Rt_f = (r * e_c).astype(BF16)
    At_f = (-kk * jnp.exp2(c - ld)).astype(BF16)
    Bt_f = bt.astype(BF16)
    Kt_f = kt.astype(BF16)
    Bh_f = (bt * e_cl).astype(BF16)
    Kh_f = (kt * e_cl).astype(BF16)
    v_b = v.astype(BF16)

    PW = 2 * HEAD
    prow = lax.broadcasted_iota(jnp.int32, (L, PW), 0)
    hcol = lax.broadcasted_iota(jnp.int32, (L, PW), 1) % HEAD
    low_incl = hcol <= prow
    low_strict = hcol < prow
    brow = lax.broadcasted_iota(jnp.int32, (2 * L, PW), 0)
    bcol = lax.broadcasted_iota(jnp.int32, (2 * L, PW), 1)
    bdmask = (brow // L) == (bcol // HEAD)
    eye2 = brow == bcol

    def pairs(x):
        per_pair = [x[:, p * PW:(p + 1) * PW].reshape(nc, 1, L, PW) for p in range(npair)]
        return jnp.concatenate(per_pair, axis=1).reshape(nc * npair, L, PW)

    Rt, At, Bt, Kt, Bh, Kh, vp = [pairs(x) for x in (Rt_f, At_f, Bt_f, Kt_f, Bh_f, Kh_f, v_b)]
    zero_p = jnp.zeros_like(vp)
    AA = _bmm_nt(jnp.concatenate([Rt, At], axis=1),
                 jnp.concatenate([_bd(Bt, bdmask), _bd(Kt, bdmask)], axis=1))
    Arb = jnp.where(low_incl, AA[:, :L, :PW], 0.0)
    Ark = jnp.where(low_incl, AA[:, :L, PW:], 0.0)
    Aab = jnp.where(low_strict, AA[:, L:, :PW], 0.0)
    Aak = jnp.where(low_strict, AA[:, L:, PW:], 0.0)
    T = _tri_inv(Aab, prow, hcol, bdmask)
    AakV = _bmm(Aak, _bd(vp, bdmask))
    WU = _bmm(T, jnp.concatenate([_bd(At, bdmask), _bd(AakV, bdmask)], axis=2))
    W, U0 = WU[:, :, :PW], WU[:, :, PW:]
    QY = _bmm(jnp.concatenate([Arb, Ark], axis=2),
              jnp.concatenate([jnp.concatenate([_bd(W, bdmask), _bd(U0, bdmask)], axis=2),
                               jnp.concatenate([jnp.zeros((nc * npair, 2 * L, PW), BF16), _bd(vp, bdmask)],
                                               axis=2)], axis=1))
    Q = Rt.astype(F32) + QY[:, :, :PW]
    Y0 = QY[:, :, PW:]
    MG = _bmm_tn(jnp.concatenate([Bh, Kh], axis=1),
                 jnp.concatenate([WU.astype(BF16), jnp.concatenate([zero_p, vp], axis=2)], axis=1))
    pl_row = pairs(e_c)[:, L - 1:L, :]
    M = jnp.where(bdmask, MG[:, :, :PW], 0.0) + jnp.where(eye2, pl_row, 0.0)
    G = jnp.where(bdmask, MG[:, :, PW:], 0.0)
    QM = jnp.concatenate([Q, M], axis=1).astype(BF16)

    H = st_scr[...]
    ys = []
    for ch in range(nc):
        sl = slice(ch * npair, (ch + 1) * npair)
        YH = _bmm(QM[sl], H)
        ys.append(YH[:, :L, :] + Y0[sl])
        H = YH[:, L:, :] + G[sl]
    st_scr[...] = H
    y = jnp.concatenate(
        [jnp.concatenate([ys[ch][p] for ch in range(nc)], axis=0) for p in range(npair)], axis=-1)
    inv_n = 1.0 / HEAD
    yc = y - _head_sums(y) * inv_n
    var = _head_sums(yc * yc) * inv_n
    yn = yc * lax.rsqrt(var + LNX_EPS)
    bonus = _head_sums(r * k * rk_ref[...])
    y_c = (yn * lng_ref[...] + lnb_ref[...] + bonus * v) * g
    m = mab_ref[...] + _sigmoid(pgc_ref[...].astype(F32) + mbc_ref[...]) * y_c
    o_ref[...] = m.astype(o_ref.dtype)


def _rwkv(p3, hv3, vf3, mab3, mu, w0, a0, kkp, kap, rkp, lng, lnb, mbc, v0, w2p, a2p, g2, v2p,
          col_r, col_k, col_v, col_wa, col_xg, col_gc, ts, hg):
    B, S, _ = p3.shape
    D = w0.shape[-1]
    cw = hg * HEAD
    layer0 = vf3 is None

    def colblk(c0, w, grouped):
        return pl.BlockSpec((None, ts, w), lambda b, gi, t: (b, t, c0 // w + (gi if grouped else 0)))

    def rowpar(c0, w, grouped):
        return pl.BlockSpec((1, w), lambda b, gi, t: (0, c0 // w + (gi if grouped else 0)))

    def matpar(rows):
        return pl.BlockSpec((rows, cw), lambda b, gi, t: (0, gi))

    act = pl.BlockSpec((None, ts, cw), lambda b, gi, t: (b, t, gi))
    in_specs = [colblk(col_r, cw, True), colblk(col_k, cw, True), colblk(col_v, cw, True),
                colblk(col_wa, LANES, False), colblk(col_xg, LANES, False), colblk(col_gc, cw, True), act]
    args = [p3, p3, p3, p3, p3, p3, mab3]
    if not layer0:
        in_specs += [pl.BlockSpec((None, ts, LANES), lambda b, gi, t: (b, t, 0)), act]
        args += [hv3, vf3]
    in_specs += [rowpar(0, cw, True), rowpar(D, cw, True), rowpar(2 * D, cw, True),
                 rowpar(3 * D, LANES, False), rowpar(3 * D + LANES, LANES, False)]
    args += [mu] * 5
    in_specs += [rowpar(0, cw, True)] * 8
    args += [w0, a0, kkp, kap, rkp, lng, lnb, mbc]
    if not layer0:
        in_specs.append(rowpar(0, cw, True))
        args.append(v0)
    in_specs += [matpar(LANES), matpar(LANES), matpar(LANES)]
    args += [w2p, a2p, g2]
    if not layer0:
        in_specs.append(matpar(LANES))
        args.append(v2p)
    out_shape = [jax.ShapeDtypeStruct((B, S, D), BF16)]
    out_specs = [act]
    if layer0:
        out_shape.append(jax.ShapeDtypeStruct((B, S, D), F32))
        out_specs.append(act)
    res = pl.pallas_call(
        functools.partial(_rwkv_kernel, ts=ts, hg=hg, layer0=layer0),
        grid=(B, D // cw, S // ts),
        in_specs=in_specs, out_specs=out_specs, out_shape=out_shape,
        scratch_shapes=[pltpu.VMEM((SUBLANES, cw), F32)] * 3
        + [pltpu.VMEM((SUBLANES, LANES), F32)] * 2
        + [pltpu.VMEM((hg // 2, 2 * HEAD, 2 * HEAD), F32)],
        compiler_params=pltpu.CompilerParams(
            dimension_semantics=("parallel", "parallel", "arbitrary"), vmem_limit_bytes=VMEM_LIMIT),
        name="rwkv",
    )(*args)
    return (res[0], res[1]) if layer0 else (res[0], vf3)


def _mlp_kernel(x_ref, m_ref, wo_ref, g2_ref, w1_ref, w2_ref, fg_ref, o_ref, acc, h2_scr, *, final):
    j = pl.program_id(1)

    @pl.when(j == 0)
    def _():
        xn = x_ref[...] + _dot(m_ref[...], wo_ref[...])
        acc[...] = xn
        h2_scr[...] = _rms(xn, g2_ref[...]).astype(BF16)

    hid = jnp.square(jnp.maximum(_dot(h2_scr[...], w1_ref[...]), 0.0))
    acc[...] += _dot(hid.astype(BF16), w2_ref[...])

    @pl.when(j == pl.num_programs(1) - 1)
    def _():
        o_ref[...] = _rms(acc[...], fg_ref[...]) if final else acc[...]


def _mlp(x2, m2, wo_bf, g2, w1_bf, w2_bf, layer, fg, tm, tf, final):
    T, D = x2.shape
    FF = w1_bf.shape[-1]
    return pl.pallas_call(
        functools.partial(_mlp_kernel, final=final),
        grid=(T // tm, FF // tf),
        in_specs=[pl.BlockSpec((tm, D), lambda i, j: (i, 0)),
                  pl.BlockSpec((tm, D), lambda i, j: (i, 0)),
                  pl.BlockSpec((None, D, D), lambda i, j: (layer, 0, 0)),
                  pl.BlockSpec((1, D), lambda i, j: (0, 0)),
                  pl.BlockSpec((None, D, tf), lambda i, j: (layer, 0, j)),
                  pl.BlockSpec((None, tf, D), lambda i, j: (layer, j, 0)),
                  pl.BlockSpec((1, D), lambda i, j: (0, 0))],
        out_specs=pl.BlockSpec((tm, D), lambda i, j: (i, 0)),
        out_shape=jax.ShapeDtypeStruct((T, D), F32),
        scratch_shapes=[pltpu.VMEM((tm, D), F32), pltpu.VMEM((tm, D), BF16)],
        compiler_params=pltpu.CompilerParams(
            dimension_semantics=("parallel", "arbitrary"), vmem_limit_bytes=VMEM_LIMIT),
        name="mlp",
    )(x2, m2, wo_bf, g2, w1_bf, w2_bf, fg)


def _tile(n, pref):
    t = min(n, pref)
    while n % t:
        t //= 2
    return t


def kernel(x, norm1_g, w_in, merge_b, conv_a_w, lru_conv_w, lru_conv_b, lru_wa, lru_ba, lru_wi, lru_bi,
           lru_a_param, rwkv_mu, rwkv_w0, rwkv_w2, rwkv_a0, rwkv_a2, rwkv_g2, rwkv_kk, rwkv_ka, rwkv_rk,
           rwkv_lnx_g, rwkv_lnx_b, rwkv_v0, rwkv_v1, rwkv_v2, w_out, norm2_g, mlp_w1, mlp_w2, final_g):
    B, S, D = x.shape
    depth = w_in.shape[0]
    T = B * S
    n_in = w_in.shape[-1]
    assert n_in == 11 * D + R_W + R_A + R_G and R_W + R_A == LANES and R_G == LANES
    assert D % (MIX_SLICES * LANES) == 0

    tm_c = _tile(T, 512)
    ts_ab = _tile(S, 1024)
    ts_c = _tile(S, 256)
    hg = 16
    tm_mlp = _tile(T, 1024)
    tf = _tile(mlp_w1.shape[-1], 1024)

    row = lambda vec: vec.reshape(1, -1)
    x2 = x.reshape(T, D)
    vf3 = None
    w_in_bf, w_out_bf = w_in.astype(BF16), w_out.astype(BF16)
    w1_bf, w2_bf = mlp_w1.astype(BF16), mlp_w2.astype(BF16)
    n_mix = N_MIX * D
    c_pieces = [(n_mix + i * D, D) for i in range(4)] + [(n_mix + 4 * D, R_W + R_A + R_G)]
    n_c = n_in - n_mix
    for l in range(depth):
        v1p = v2p = None
        if l > 0:
            v1p = jnp.pad(rwkv_v1[l - 1], ((0, 0), (0, LANES - R_V))).astype(BF16)
            v2p = jnp.pad(rwkv_v2[l - 1], ((0, LANES - R_V), (0, 0))).astype(BF16)
        w2p = jnp.pad(rwkv_w2[l], ((0, R_A), (0, 0))).astype(BF16)
        a2p = jnp.pad(rwkv_a2[l], ((R_W, 0), (0, 0))).astype(BF16)
        g2 = rwkv_g2[l].astype(BF16)
        bw = D // LRU_HEADS
        wa = lru_wa[l].reshape(LRU_HEADS // 2, 2, bw, bw)
        wi = lru_wi[l].reshape(LRU_HEADS // 2, 2, bw, bw)
        z = jnp.zeros_like(wa[:, 0])
        blockdiag = lambda w: jnp.concatenate(
            [jnp.concatenate([w[:, 0], z], axis=-1), jnp.concatenate([z, w[:, 1]], axis=-1)], axis=-2)
        wg = jnp.concatenate([blockdiag(wa), blockdiag(wi)], axis=-1).astype(BF16)
        mb = merge_b[l]

        mab3 = _inproj_mix(x2.reshape(B, S, D), row(norm1_g[l]), w_in_bf, l, conv_a_w[l], lru_conv_w[l],
                           row(lru_conv_b[l]), wg, row(lru_ba[l]), row(lru_bi[l]), row(lru_a_param[l]),
                           row(mb[:D]), row(mb[D:2 * D]), ts_ab)
        pc2, hv2 = _inproj(x2, row(norm1_g[l]), w_in_bf, l, c_pieces, v1p, tm_c)
        pc3 = pc2.reshape(B, S, n_c)
        m3, vf3 = _rwkv(pc3, None if l == 0 else hv2.reshape(B, S, LANES), vf3, mab3,
                        row(rwkv_mu[l]), row(rwkv_w0[l]), row(rwkv_a0[l]), row(rwkv_kk[l]), row(rwkv_ka[l]),
                        row(rwkv_rk[l]), row(rwkv_lnx_g[l]), row(rwkv_lnx_b[l]), row(mb[2 * D:]),
                        None if l == 0 else row(rwkv_v0[l - 1]), w2p, a2p, g2, v2p,
                        D, 2 * D, 3 * D, 4 * D, 4 * D + R_W + R_A, 0, ts_c, hg)
        x2 = _mlp(x2, m3.reshape(T, D), w_out_bf, row(norm2_g[l]), w1_bf, w2_bf, l, row(final_g),
                  tm_mlp, tf, final=(l == depth - 1))
    return x2.reshape(B, S, D)
```

```python
import functools

import jax
import jax.numpy as jnp
from jax import lax
from jax.experimental import pallas as pl
from jax.experimental.pallas import tpu as pltpu

F32 = jnp.float32
BF16 = jnp.bfloat16

EPS = 1e-6
LRU_C = 8.0
LRU_HEADS = 16
HEAD = 64
CHUNK = 64
LNX_EPS = HEAD * 1e-5
EXP_M05 = 0.6065306597126334
LOG2E = 1.4426950408889634
R_W, R_A, R_G, R_V = 64, 64, 128, 32
LANES = 128
SUBLANES = 8
VMEM_LIMIT = 48 * 1024 * 1024


def _dot(a, b):
    return jnp.dot(a, b, preferred_element_type=F32)


def _sigmoid(x):
    return 1.0 / (1.0 + jnp.exp(-x))


def _softplus(x):
    return jnp.maximum(x, 0.0) + jnp.log1p(jnp.exp(-jnp.abs(x)))


def _rms(x, g):
    return x * lax.rsqrt(jnp.mean(x * x, axis=-1, keepdims=True) + EPS) * g


def _head_sums(x):
    lo = lax.broadcasted_iota(jnp.int32, (x.shape[0], LANES), 1) < HEAD
    cols = []
    for i in range(0, x.shape[1], LANES):
        xi = x[:, i:i + LANES]
        s_lo = jnp.sum(jnp.where(lo, xi, 0.0), axis=-1, keepdims=True)
        s_hi = jnp.sum(xi, axis=-1, keepdims=True) - s_lo
        cols.append(jnp.where(lo, s_lo, s_hi))
    return cols[0] if len(cols) == 1 else jnp.concatenate(cols, axis=1)


def _sum_left(ones_bf, x):
    p1 = x.astype(BF16)
    p2 = (x - p1.astype(F32)).astype(BF16)
    return _dot(ones_bf, p1) + _dot(ones_bf, p2)


def _shifted(tail_ref, x, j, ts):
    w = x.shape[-1]
    x3 = x.reshape(ts // SUBLANES, SUBLANES, w)
    tail3 = tail_ref[...].reshape(1, SUBLANES, w)
    sub = lax.broadcasted_iota(jnp.int32, (1, SUBLANES, w), 1)
    out = []
    for d in range(1, j + 1):
        rot = pltpu.roll(x3, d, 1)
        prev_group = jnp.concatenate([pltpu.roll(tail3, d, 1), rot[:-1]], axis=0)
        out.append(jnp.where(sub < d, prev_group, rot).reshape(ts, w))
    tail_ref[...] = x[ts - SUBLANES:, :]
    return out


N_MIX = 7
MIX_SLICES = 2


def _mix_slice(cols, first_tile, cwa, cwb, cbb, wg, bia, bii, ap, mba, mbb,
               ztail, xbtail, hcar, hs_scr, as_scr, ent_scr, ts):
    ba, ca, xa, xb, gb, ga, gbb = [c.astype(F32) for c in cols]

    z = ca * xa
    z1, z2 = _shifted(ztail, z, 2, ts)
    y_a = ba * (z * cwa[2:3] + z1 * cwa[1:2] + z2 * cwa[0:1])

    x1, x2, x3 = _shifted(xbtail, xb, 3, ts)
    u = xb * cwb[3:4] + x1 * cwb[2:3] + x2 * cwb[1:2] + x3 * cwb[0:1] + cbb

    res = _dot(u.astype(BF16), wg)
    gate_a = _sigmoid(res[:, :LANES] + bia)
    gate_i = _sigmoid(res[:, LANES:] + bii)
    log_a = (-LRU_C) * gate_a * _softplus(ap)
    a = jnp.exp(log_a)
    m2 = -jnp.tanh(log_a) * (1.0 + a * a)
    mult = jnp.where(m2 > 0.0, m2 * lax.rsqrt(m2), 0.0)
    row = lax.broadcasted_iota(jnp.int32, (ts, LANES), 0)
    mult = jnp.where((row == 0) & first_tile, 1.0, mult)
    h = u * gate_i * mult

    ng = ts // SUBLANES
    hs_scr[...] = h
    as_scr[...] = a
    cum_a, loc_h = [], []
    for s in range(SUBLANES):
        a_s = as_scr[pl.ds(s, ng, stride=SUBLANES), :]
        u_s = hs_scr[pl.ds(s, ng, stride=SUBLANES), :]
        cum_a.append(a_s * cum_a[-1] if s else a_s)
        loc_h.append(a_s * loc_h[-1] + u_s if s else u_s)
    carry = hcar[...]
    for gi in range(ng):
        ent_scr[gi:gi + 1, :] = carry
        carry = loc_h[-1][gi:gi + 1] + cum_a[-1][gi:gi + 1] * carry
    hcar[...] = carry
    enter = ent_scr[...]
    for s in range(SUBLANES):
        hs_scr[pl.ds(s, ng, stride=SUBLANES), :] = loc_h[s] + cum_a[s] * enter
    h = hs_scr[...]

    gelu = 0.5 * gb * (1.0 + jnp.tanh(0.7978845608028654 * (gb + 0.044715 * gb * gb * gb)))
    return _sigmoid(ga + mba) * y_a + _sigmoid(gbb + mbb) * (h * gelu)


def _inproj_mix_kernel(*refs, ts, nt):
    x_ref, g_ref = refs[:2]
    w_refs = refs[2:2 + N_MIX]
    (cwa_ref, cwb_ref, cbb_ref, wg_ref, bia_ref, bii_ref, ap_ref, mba_ref, mbb_ref, o_ref,
     h_scr, pq_scr, ztail, xbtail, hcar, hs_scr, as_scr, ent_scr) = refs[2 + N_MIX:]
    t = pl.program_id(1)
    q = pl.program_id(2)
    gw = MIX_SLICES * LANES

    @pl.when((q == 0) & (t < nt))
    def _():
        h_scr[...] = _rms(x_ref[...], g_ref[...]).astype(BF16)

    @pl.when(t == 1)
    def _():
        ztail[q] = jnp.zeros(ztail.shape[1:], F32)
        xbtail[q] = jnp.zeros(xbtail.shape[1:], F32)
        hcar[q] = jnp.zeros(hcar.shape[1:], F32)

    def mix():
        pq = pq_scr[q]
        for i in range(MIX_SLICES):
            lanes = slice(i * LANES, (i + 1) * LANES)
            cols = [pq[:, g * gw + i * LANES:g * gw + (i + 1) * LANES] for g in range(N_MIX)]
            o_ref[:, lanes] = _mix_slice(
                cols, t == 1, cwa_ref[:, lanes], cwb_ref[:, lanes], cbb_ref[:, lanes],
                wg_ref[i], bia_ref[:, lanes], bii_ref[:, lanes], ap_ref[:, lanes], mba_ref[:, lanes],
                mbb_ref[:, lanes], ztail.at[q, i], xbtail.at[q, i], hcar.at[q, i],
                hs_scr.at[i], as_scr.at[i], ent_scr.at[i], ts)

    def project():
        hb = h_scr[...]
        for g in range(N_MIX):
            pq_scr[q, :, g * gw:(g + 1) * gw] = _dot(hb, w_refs[g][...]).astype(BF16)

    @pl.when(t == 0)
    def _():
        o_ref[...] = jnp.zeros_like(o_ref)
        project()

    @pl.when((t > 0) & (t < nt))
    def _():
        mix()
        project()

    @pl.when(t == nt)
    def _():
        mix()


def _inproj_mix(x3, g, w_bf, layer, cwa, cwb, cbb, wg, bia, bii, ap, mba, mbb, ts):
    B, S, D = x3.shape
    gw = MIX_SLICES * LANES
    nq = D // gw
    nt = S // ts
    wq = N_MIX * gw

    def chan(rows):
        return pl.BlockSpec((rows, gw), lambda b, t, q: (0, q))

    sl = MIX_SLICES
    return pl.pallas_call(
        functools.partial(_inproj_mix_kernel, ts=ts, nt=nt),
        grid=(B, nt + 1, nq),
        in_specs=[pl.BlockSpec((None, ts, D), lambda b, t, q: (b, jnp.minimum(t, nt - 1), 0)),
                  pl.BlockSpec((1, D), lambda b, t, q: (0, 0))]
        + [pl.BlockSpec((None, D, gw), lambda b, t, q, grp=grp: (layer, 0, grp * nq + q)) for grp in range(N_MIX)]
        + [chan(cwa.shape[0]), chan(cwb.shape[0]), chan(1),
                  pl.BlockSpec((sl, LANES, 2 * LANES), lambda b, t, q: (q, 0, 0)),
                  chan(1), chan(1), chan(1), chan(1), chan(1)],
        out_specs=pl.BlockSpec((None, ts, gw),
                               lambda b, t, q: (b, jnp.maximum(t - 1, 0), jnp.where(t == 0, 0, q))),
        out_shape=jax.ShapeDtypeStruct((B, S, D), F32),
        scratch_shapes=[pltpu.VMEM((ts, D), BF16), pltpu.VMEM((nq, ts, wq), BF16),
                        pltpu.VMEM((nq, sl, SUBLANES, LANES), F32), pltpu.VMEM((nq, sl, SUBLANES, LANES), F32),
                        pltpu.VMEM((nq, sl, 1, LANES), F32),
                        pltpu.VMEM((sl, ts, LANES), F32), pltpu.VMEM((sl, ts, LANES), F32),
                        pltpu.VMEM((sl, ts // SUBLANES, LANES), F32)],
        compiler_params=pltpu.CompilerParams(
            dimension_semantics=("parallel", "arbitrary", "arbitrary"), vmem_limit_bytes=VMEM_LIMIT),
        name="inproj_mix",
    )(x3, g, *([w_bf] * N_MIX), cwa, cwb, cbb, wg, bia, bii, ap, mba, mbb)


def _bmm(a, b):
    return jnp.einsum('nik,nkj->nij', a.astype(BF16), b.astype(BF16), preferred_element_type=F32)


def _bmm_nt(a, b):
    return jnp.einsum('nik,njk->nij', a.astype(BF16), b.astype(BF16), preferred_element_type=F32)


def _bmm_tn(a, b):
    return jnp.einsum('nti,ntj->nij', a.astype(BF16), b.astype(BF16), preferred_element_type=F32)


def _bd(x, bdmask):
    xb = x.astype(BF16)
    return jnp.where(bdmask, jnp.concatenate([xb, xb], axis=1), jnp.zeros((), BF16))


def _tri_inv(A, row, hcol, bdmask):
    def blk(s):
        return (row // s) == (hcol // s)

    def pmm(x, y):
        return _bmm(x, _bd(y, bdmask))

    L = A.shape[1]
    eye = (row == hcol).astype(F32)
    Ad = jnp.where(blk(SUBLANES), A, 0.0)
    T = eye + Ad
    P = Ad
    Pbd = _bd(P, bdmask)
    s = 2
    while s < SUBLANES:
        P = _bmm(P, Pbd)
        Pbd = _bd(P, bdmask)
        T = T + _bmm(T, Pbd)
        s *= 2
    s = SUBLANES
    while s < L:
        Aoff = jnp.where(blk(2 * s) & jnp.logical_not(blk(s)), A, 0.0)
        slabs = [T[:, i * s:(i + 1) * s] for i in range(L // s)]
        low = jnp.concatenate(slabs[1::2], axis=1)
        low = low + pmm(pmm(low, Aoff), T)
        for idx, i in enumerate(range(1, L // s, 2)):
            slabs[i] = low[:, idx * s:(idx + 1) * s]
        T = jnp.concatenate(slabs, axis=1)
        s *= 2
    return T


def _rwkv_kernel(*refs, ts, hg, layer0):
    it = iter(refs)
    x_ref, g1_ref, wgc_ref, wr_ref, wk_ref, wv_ref, ws_ref, mab_ref = [next(it) for _ in range(8)]
    if not layer0:
        v1_ref, vf_ref = next(it), next(it)
    (mur_ref, muk_ref, muv_ref, muwa_ref, mug_ref, w0_ref, a0_ref, kk_ref, ka_ref, rk_ref,
     lng_ref, lnb_ref, mbc_ref) = [next(it) for _ in range(13)]
    if not layer0:
        v0_ref = next(it)
    w2_ref, a2_ref, g2_ref = next(it), next(it), next(it)
    if not layer0:
        v2_ref = next(it)
    o_ref = next(it)
    if layer0:
        vfo_ref = next(it)
    rpad, kpad, vpad, wapad, xgpad, st_scr = [next(it) for _ in range(6)]

    t = pl.program_id(2)
    cw = hg * HEAD
    nc = ts // CHUNK
    npair = hg // 2
    L = CHUNK

    @pl.when(t == 0)
    def _():
        for pad in (rpad, kpad, vpad, wapad, xgpad):
            pad[...] = jnp.zeros_like(pad)
        st_scr[...] = jnp.zeros_like(st_scr)

    ri = lax.broadcasted_iota(jnp.int32, (ts, ts), 0)
    ci = lax.broadcasted_iota(jnp.int32, (ts, ts), 1)

    hb = _rms(x_ref[...], g1_ref[...]).astype(BF16)
    p_small = _dot(hb, ws_ref[...])

    def token_shift(pad, x, mu_ref):
        (prev,) = _shifted(pad, x, 1, ts)
        return x + (prev - x) * mu_ref[...]

    r = token_shift(rpad, _dot(hb, wr_ref[...]), mur_ref)
    k = token_shift(kpad, _dot(hb, wk_ref[...]), muk_ref)
    v = token_shift(vpad, _dot(hb, wv_ref[...]), muv_ref)
    xwa = token_shift(wapad, p_small[:, :LANES], muwa_ref)
    xg = token_shift(xgpad, p_small[:, LANES:], mug_ref)

    wl = w0_ref[...] + _dot(jnp.tanh(xwa).astype(BF16), w2_ref[...])
    ld = (-LOG2E * EXP_M05) * _sigmoid(wl)
    a = _sigmoid(a0_ref[...] + _dot(xwa.astype(BF16), a2_ref[...]))
    g = _dot(_sigmoid(xg).astype(BF16), g2_ref[...])
    if layer0:
        vfo_ref[...] = v
    else:
        hv = _dot(hb, v1_ref[...])
        mix = _sigmoid(v0_ref[...] + _dot(hv.astype(BF16), v2_ref[...]))
        v = v + (vf_ref[...] - v) * mix

    ri = lax.broadcasted_iota(jnp.int32, (ts, ts), 0)
    ci = lax.broadcasted_iota(jnp.int32, (ts, ts), 1)
    tri = jnp.where(((ri // L) == (ci // L)) & (ci <= ri), 1.0, 0.0).astype(BF16)

    c = _sum_left(tri, ld)
    e_c = jnp.exp2(c)
    e_n = 1.0 / e_c
    e_cl = jnp.broadcast_to(e_c.reshape(nc, L, cw)[:, L - 1:L, :], (nc, L, cw)).reshape(ts, cw)

    kk = k * kk_ref[...]
    kk = kk * lax.rsqrt(jnp.maximum(_head_sums(kk * kk), 1e-24))
    k = k * (1.0 + (a - 1.0) * ka_ref[...])
    bt = kk * a * e_n
    kt = k * e_n
    Rt_f = (r * e_c).astype(BF16)
    At_f = (-kk * jnp.exp2(c - ld)).astype(BF16)
    Bt_f = bt.astype(BF16)
    Kt_f = kt.astype(BF16)
    Bh_f = (bt * e_cl).astype(BF16)
    Kh_f = (kt * e_cl).astype(BF16)
    v_b = v.astype(BF16)

    PW = 2 * HEAD
    prow = lax.broadcasted_iota(jnp.int32, (L, PW), 0)
    hcol = lax.broadcasted_iota(jnp.int32, (L, PW), 1) % HEAD
    low_incl = hcol <= prow
    low_strict = hcol < prow
    brow = lax.broadcasted_iota(jnp.int32, (2 * L, PW), 0)
    bcol = lax.broadcasted_iota(jnp.int32, (2 * L, PW), 1)
    bdmask = (brow // L) == (bcol // HEAD)
    eye2 = brow == bcol

    def pairs(x):
        per_pair = [x[:, p * PW:(p + 1) * PW].reshape(nc, 1, L, PW) for p in range(npair)]
        return jnp.concatenate(per_pair, axis=1).reshape(nc * npair, L, PW)

    Rt, At, Bt, Kt, Bh, Kh, vp = [pairs(x) for x in (Rt_f, At_f, Bt_f, Kt_f, Bh_f, Kh_f, v_b)]
    zero_p = jnp.zeros_like(vp)
    AA = _bmm_nt(jnp.concatenate([Rt, At], axis=1),
                 jnp.concatenate([_bd(Bt, bdmask), _bd(Kt, bdmask)], axis=1))
    Arb = jnp.where(low_incl, AA[:, :L, :PW], 0.0)
    Ark = jnp.where(low_incl, AA[:, :L, PW:], 0.0)
    Aab = jnp.where(low_strict, AA[:, L:, :PW], 0.0)
    Aak = jnp.where(low_strict, AA[:, L:, PW:], 0.0)
    T = _tri_inv(Aab, prow, hcol, bdmask)
    AakV = _bmm(Aak, _bd(vp, bdmask))
    WU = _bmm(T, jnp.concatenate([_bd(At, bdmask), _bd(AakV, bdmask)], axis=2))
    W, U0 = WU[:, :, :PW], WU[:, :, PW:]
    QY = _bmm(jnp.concatenate([Arb, Ark], axis=2),
              jnp.concatenate([jnp.concatenate([_bd(W, bdmask), _bd(U0, bdmask)], axis=2),
                               jnp.concatenate([jnp.zeros((nc * npair, 2 * L, PW), BF16), _bd(vp, bdmask)],
                                               axis=2)], axis=1))
    Q = Rt.astype(F32) + QY[:, :, :PW]
    Y0 = QY[:, :, PW:]
    MG = _bmm_tn(jnp.concatenate([Bh, Kh], axis=1),
                 jnp.concatenate([WU.astype(BF16), jnp.concatenate([zero_p, vp], axis=2)], axis=1))
    pl_row = pairs(e_c)[:, L - 1:L, :]
    M = jnp.where(bdmask, MG[:, :, :PW], 0.0) + jnp.where(eye2, pl_row, 0.0)
    G = jnp.where(bdmask, MG[:, :, PW:], 0.0)
    QM = jnp.concatenate([Q, M], axis=1).astype(BF16)

    H = st_scr[...]
    ys = []
    for ch in range(nc):
        sl = slice(ch * npair, (ch + 1) * npair)
        YH = _bmm(QM[sl], H)
        ys.append(YH[:, :L, :] + Y0[sl])
        H = YH[:, L:, :] + G[sl]
    st_scr[...] = H
    y = jnp.concatenate(
        [jnp.concatenate([ys[ch][p] for ch in range(nc)], axis=0) for p in range(npair)], axis=-1)
    inv_n = 1.0 / HEAD
    yc = y - _head_sums(y) * inv_n
    var = _head_sums(yc * yc) * inv_n
    yn = yc * lax.rsqrt(var + LNX_EPS)
    bonus = _head_sums(r * k * rk_ref[...])
    y_c = (yn * lng_ref[...] + lnb_ref[...] + bonus * v) * g
    m = mab_ref[...] + _sigmoid(_dot(hb, wgc_ref[...]) + mbc_ref[...]) * y_c
    o_ref[...] = m.astype(o_ref.dtype)


def _rwkv(x3, g1, w_bf, layer, col_c, v1p, vf3, mab3, mu, w0, a0, kkp, kap, rkp, lng, lnb, mbc, v0,
          w2p, a2p, g2, v2p, ts, hg):
    B, S, D = x3.shape
    cw = hg * HEAD
    assert cw == D
    layer0 = vf3 is None

    def wcols(c0, w):
        return pl.BlockSpec((None, D, w), lambda b, gi, t: (layer, 0, c0 // w))

    def rowpar(c0, w, grouped):
        return pl.BlockSpec((1, w), lambda b, gi, t: (0, c0 // w + (gi if grouped else 0)))

    def matpar(rows):
        return pl.BlockSpec((rows, cw), lambda b, gi, t: (0, gi))

    act = pl.BlockSpec((None, ts, cw), lambda b, gi, t: (b, t, gi))
    in_specs = [act, pl.BlockSpec((1, D), lambda b, gi, t: (0, 0))]
    in_specs += [wcols(col_c + i * D, D) for i in range(4)] + [wcols(col_c + 4 * D, R_W + R_A + R_G), act]
    args = [x3, g1] + [w_bf] * 5 + [mab3]
    if not layer0:
        in_specs += [pl.BlockSpec((D, LANES), lambda b, gi, t: (0, 0)), act]
        args += [v1p, vf3]
    in_specs += [rowpar(0, cw, True), rowpar(D, cw, True), rowpar(2 * D, cw, True),
                 rowpar(3 * D, LANES, False), rowpar(3 * D + LANES, LANES, False)]
    args += [mu] * 5
    in_specs += [rowpar(0, cw, True)] * 8
    args += [w0, a0, kkp, kap, rkp, lng, lnb, mbc]
    if not layer0:
        in_specs.append(rowpar(0, cw, True))
        args.append(v0)
    in_specs += [matpar(LANES), matpar(LANES), matpar(LANES)]
    args += [w2p, a2p, g2]
    if not layer0:
        in_specs.append(matpar(LANES))
        args.append(v2p)
    out_shape = [jax.ShapeDtypeStruct((B, S, D), BF16)]
    out_specs = [act]
    if layer0:
        out_shape.append(jax.ShapeDtypeStruct((B, S, D), F32))
        out_specs.append(act)
    res = pl.pallas_call(
        functools.partial(_rwkv_kernel, ts=ts, hg=hg, layer0=layer0),
        grid=(B, D // cw, S // ts),
        in_specs=in_specs, out_specs=out_specs, out_shape=out_shape,
        scratch_shapes=[pltpu.VMEM((SUBLANES, cw), F32)] * 3
        + [pltpu.VMEM((SUBLANES, LANES), F32)] * 2
        + [pltpu.VMEM((hg // 2, 2 * HEAD, 2 * HEAD), F32)],
        compiler_params=pltpu.CompilerParams(
            dimension_semantics=("parallel", "parallel", "arbitrary"), vmem_limit_bytes=VMEM_LIMIT),
        name="rwkv",
    )(*args)
    return (res[0], res[1]) if layer0 else (res[0], vf3)


def _mlp_kernel(x_ref, m_ref, wo_ref, g2_ref, w1_ref, w2_ref, fg_ref, o_ref, acc, h2_scr, *, final):
    j = pl.program_id(1)

    @pl.when(j == 0)
    def _():
        xn = x_ref[...] + _dot(m_ref[...], wo_ref[...])
        acc[...] = xn
        h2_scr[...] = _rms(xn, g2_ref[...]).astype(BF16)

    hid = jnp.square(jnp.maximum(_dot(h2_scr[...], w1_ref[...]), 0.0))
    acc[...] += _dot(hid.astype(BF16), w2_ref[...])

    @pl.when(j == pl.num_programs(1) - 1)
    def _():
        o_ref[...] = _rms(acc[...], fg_ref[...]) if final else acc[...]


def _mlp(x2, m2, wo_bf, g2, w1_bf, w2_bf, layer, fg, tm, tf, final):
    T, D = x2.shape
    FF = w1_bf.shape[-1]
    return pl.pallas_call(
        functools.partial(_mlp_kernel, final=final),
        grid=(T // tm, FF // tf),
        in_specs=[pl.BlockSpec((tm, D), lambda i, j: (i, 0)),
                  pl.BlockSpec((tm, D), lambda i, j: (i, 0)),
                  pl.BlockSpec((None, D, D), lambda i, j: (layer, 0, 0)),
                  pl.BlockSpec((1, D), lambda i, j: (0, 0)),
                  pl.BlockSpec((None, D, tf), lambda i, j: (layer, 0, j)),
                  pl.BlockSpec((None, tf, D), lambda i, j: (layer, j, 0)),
                  pl.BlockSpec((1, D), lambda i, j: (0, 0))],
        out_specs=pl.BlockSpec((tm, D), lambda i, j: (i, 0)),
        out_shape=jax.ShapeDtypeStruct((T, D), F32),
        scratch_shapes=[pltpu.VMEM((tm, D), F32), pltpu.VMEM((tm, D), BF16)],
        compiler_params=pltpu.CompilerParams(
            dimension_semantics=("parallel", "arbitrary"), vmem_limit_bytes=VMEM_LIMIT),
        name="mlp",
    )(x2, m2, wo_bf, g2, w1_bf, w2_bf, fg)


def _tile(n, pref):
    t = min(n, pref)
    while n % t:
        t //= 2
    return t


def kernel(x, norm1_g, w_in, merge_b, conv_a_w, lru_conv_w, lru_conv_b, lru_wa, lru_ba, lru_wi, lru_bi,
           lru_a_param, rwkv_mu, rwkv_w0, rwkv_w2, rwkv_a0, rwkv_a2, rwkv_g2, rwkv_kk, rwkv_ka, rwkv_rk,
           rwkv_lnx_g, rwkv_lnx_b, rwkv_v0, rwkv_v1, rwkv_v2, w_out, norm2_g, mlp_w1, mlp_w2, final_g):
    B, S, D = x.shape
    depth = w_in.shape[0]
    T = B * S
    n_in = w_in.shape[-1]
    assert n_in == 11 * D + R_W + R_A + R_G and R_W + R_A == LANES and R_G == LANES
    assert D % (MIX_SLICES * LANES) == 0

    ts_ab = _tile(S, 1024)
    ts_c = _tile(S, 256)
    hg = 16
    tm_mlp = _tile(T, 1024)
    tf = _tile(mlp_w1.shape[-1], 1024)

    row = lambda vec: vec.reshape(1, -1)
    x2 = x.reshape(T, D)
    vf3 = None
    w_in_bf, w_out_bf = w_in.astype(BF16), w_out.astype(BF16)
    w1_bf, w2_bf = mlp_w1.astype(BF16), mlp_w2.astype(BF16)
    n_mix = N_MIX * D
    for l in range(depth):
        v1p = v2p = None
        if l > 0:
            v1p = jnp.pad(rwkv_v1[l - 1], ((0, 0), (0, LANES - R_V))).astype(BF16)
            v2p = jnp.pad(rwkv_v2[l - 1], ((0, LANES - R_V), (0, 0))).astype(BF16)
        w2p = jnp.pad(rwkv_w2[l], ((0, R_A), (0, 0))).astype(BF16)
        a2p = jnp.pad(rwkv_a2[l], ((R_W, 0), (0, 0))).astype(BF16)
        g2 = rwkv_g2[l].astype(BF16)
        bw = D // LRU_HEADS
        wa = lru_wa[l].reshape(LRU_HEADS // 2, 2, bw, bw)
        wi = lru_wi[l].reshape(LRU_HEADS // 2, 2, bw, bw)
        z = jnp.zeros_like(wa[:, 0])
        blockdiag = lambda w: jnp.concatenate(
            [jnp.concatenate([w[:, 0], z], axis=-1), jnp.concatenate([z, w[:, 1]], axis=-1)], axis=-2)
        wg = jnp.concatenate([blockdiag(wa), blockdiag(wi)], axis=-1).astype(BF16)
        mb = merge_b[l]

        mab3 = _inproj_mix(x2.reshape(B, S, D), row(norm1_g[l]), w_in_bf, l, conv_a_w[l], lru_conv_w[l],
                           row(lru_conv_b[l]), wg, row(lru_ba[l]), row(lru_bi[l]), row(lru_a_param[l]),
                           row(mb[:D]), row(mb[D:2 * D]), ts_ab)
        m3, vf3 = _rwkv(x2.reshape(B, S, D), row(norm1_g[l]), w_in_bf, l, n_mix, v1p, vf3, mab3,
                        row(rwkv_mu[l]), row(rwkv_w0[l]), row(rwkv_a0[l]), row(rwkv_kk[l]), row(rwkv_ka[l]),
                        row(rwkv_rk[l]), row(rwkv_lnx_g[l]), row(rwkv_lnx_b[l]), row(mb[2 * D:]),
                        None if l == 0 else row(rwkv_v0[l - 1]), w2p, a2p, g2, v2p, ts_c, hg)
        x2 = _mlp(x2, m3.reshape(T, D), w_out_bf, row(norm2_g[l]), w1_bf, w2_bf, l, row(final_g),
                  tm_mlp, tf, final=(l == depth - 1))
    return x2.reshape(B, S, D)
```

```python
import functools

import jax
import jax.numpy as jnp
from jax import lax
from jax.experimental import pallas as pl
from jax.experimental.pallas import tpu as pltpu

F32 = jnp.float32
BF16 = jnp.bfloat16

EPS = 1e-6
LRU_C = 8.0
LRU_HEADS = 16
HEAD = 64
CHUNK = 64
LNX_EPS = HEAD * 1e-5
EXP_M05 = 0.6065306597126334
LOG2E = 1.4426950408889634
R_W, R_A, R_G, R_V = 64, 64, 128, 32
LANES = 128
SUBLANES = 8
VMEM_LIMIT = 48 * 1024 * 1024


def _dot(a, b):
    return jnp.dot(a, b, preferred_element_type=F32)


def _sigmoid(x):
    return 1.0 / (1.0 + jnp.exp(-x))


def _softplus(x):
    return jnp.maximum(x, 0.0) + jnp.log1p(jnp.exp(-jnp.abs(x)))


def _rms(x, g):
    return x * lax.rsqrt(jnp.mean(x * x, axis=-1, keepdims=True) + EPS) * g


def _head_sums(x):
    lo = lax.broadcasted_iota(jnp.int32, (x.shape[0], LANES), 1) < HEAD
    cols = []
    for i in range(0, x.shape[1], LANES):
        xi = x[:, i:i + LANES]
        s_lo = jnp.sum(jnp.where(lo, xi, 0.0), axis=-1, keepdims=True)
        s_hi = jnp.sum(xi, axis=-1, keepdims=True) - s_lo
        cols.append(jnp.where(lo, s_lo, s_hi))
    return cols[0] if len(cols) == 1 else jnp.concatenate(cols, axis=1)


def _sum_left(ones_bf, x):
    p1 = x.astype(BF16)
    p2 = (x - p1.astype(F32)).astype(BF16)
    return _dot(ones_bf, p1) + _dot(ones_bf, p2)


def _shifted(tail_ref, x, j, ts):
    w = x.shape[-1]
    x3 = x.reshape(ts // SUBLANES, SUBLANES, w)
    tail3 = tail_ref[...].reshape(1, SUBLANES, w)
    sub = lax.broadcasted_iota(jnp.int32, (1, SUBLANES, w), 1)
    out = []
    for d in range(1, j + 1):
        rot = pltpu.roll(x3, d, 1)
        prev_group = jnp.concatenate([pltpu.roll(tail3, d, 1), rot[:-1]], axis=0)
        out.append(jnp.where(sub < d, prev_group, rot).reshape(ts, w))
    tail_ref[...] = x[ts - SUBLANES:, :]
    return out


N_MIX = 7
MIX_SLICES = 2


def _mix_slice(cols, first_tile, cwa, cwb, cbb, wg, bia, bii, ap, mba, mbb,
               ztail, xbtail, hcar, hs_scr, as_scr, ent_scr, ts):
    ba, ca, xa, xb, gb, ga, gbb = [c.astype(F32) for c in cols]

    z = ca * xa
    z1, z2 = _shifted(ztail, z, 2, ts)
    y_a = ba * (z * cwa[2:3] + z1 * cwa[1:2] + z2 * cwa[0:1])

    x1, x2, x3 = _shifted(xbtail, xb, 3, ts)
    u = xb * cwb[3:4] + x1 * cwb[2:3] + x2 * cwb[1:2] + x3 * cwb[0:1] + cbb

    res = _dot(u.astype(BF16), wg)
    gate_a = _sigmoid(res[:, :LANES] + bia)
    gate_i = _sigmoid(res[:, LANES:] + bii)
    log_a = (-LRU_C) * gate_a * _softplus(ap)
    a = jnp.exp(log_a)
    m2 = -jnp.tanh(log_a) * (1.0 + a * a)
    mult = jnp.where(m2 > 0.0, m2 * lax.rsqrt(m2), 0.0)
    row = lax.broadcasted_iota(jnp.int32, (ts, LANES), 0)
    mult = jnp.where((row == 0) & first_tile, 1.0, mult)
    h = u * gate_i * mult

    ng = ts // SUBLANES
    hs_scr[...] = h
    as_scr[...] = a
    cum_a, loc_h = [], []
    for s in range(SUBLANES):
        a_s = as_scr[pl.ds(s, ng, stride=SUBLANES), :]
        u_s = hs_scr[pl.ds(s, ng, stride=SUBLANES), :]
        cum_a.append(a_s * cum_a[-1] if s else a_s)
        loc_h.append(a_s * loc_h[-1] + u_s if s else u_s)
    carry = hcar[...]
    for gi in range(ng):
        ent_scr[gi:gi + 1, :] = carry
        carry = loc_h[-1][gi:gi + 1] + cum_a[-1][gi:gi + 1] * carry
    hcar[...] = carry
    enter = ent_scr[...]
    for s in range(SUBLANES):
        hs_scr[pl.ds(s, ng, stride=SUBLANES), :] = loc_h[s] + cum_a[s] * enter
    h = hs_scr[...]

    gelu = 0.5 * gb * (1.0 + jnp.tanh(0.7978845608028654 * (gb + 0.044715 * gb * gb * gb)))
    return _sigmoid(ga + mba) * y_a + _sigmoid(gbb + mbb) * (h * gelu)


def _inproj_mix_kernel(*refs, ts, nt):
    x_ref, g_ref = refs[:2]
    w_refs = refs[2:2 + N_MIX]
    (cwa_ref, cwb_ref, cbb_ref, wg_ref, bia_ref, bii_ref, ap_ref, mba_ref, mbb_ref, o_ref,
     h_scr, pq_scr, ztail, xbtail, hcar, hs_scr, as_scr, ent_scr) = refs[2 + N_MIX:]
    t = pl.program_id(1)
    q = pl.program_id(2)
    gw = MIX_SLICES * LANES

    @pl.when((q == 0) & (t < nt))
    def _():
        h_scr[...] = _rms(x_ref[...], g_ref[...]).astype(BF16)

    @pl.when(t == 1)
    def _():
        ztail[q] = jnp.zeros(ztail.shape[1:], F32)
        xbtail[q] = jnp.zeros(xbtail.shape[1:], F32)
        hcar[q] = jnp.zeros(hcar.shape[1:], F32)

    def mix():
        pq = pq_scr[q]
        for i in range(MIX_SLICES):
            lanes = slice(i * LANES, (i + 1) * LANES)
            cols = [pq[:, g * gw + i * LANES:g * gw + (i + 1) * LANES] for g in range(N_MIX)]
            o_ref[:, lanes] = _mix_slice(
                cols, t == 1, cwa_ref[:, lanes], cwb_ref[:, lanes], cbb_ref[:, lanes],
                wg_ref[i], bia_ref[:, lanes], bii_ref[:, lanes], ap_ref[:, lanes], mba_ref[:, lanes],
                mbb_ref[:, lanes], ztail.at[q, i], xbtail.at[q, i], hcar.at[q, i],
                hs_scr.at[i], as_scr.at[i], ent_scr.at[i], ts).astype(o_ref.dtype)

    def project():
        hb = h_scr[...]
        for g in range(N_MIX):
            pq_scr[q, :, g * gw:(g + 1) * gw] = _dot(hb, w_refs[g][...]).astype(BF16)

    @pl.when(t == 0)
    def _():
        o_ref[...] = jnp.zeros_like(o_ref)
        project()

    @pl.when((t > 0) & (t < nt))
    def _():
        mix()
        project()

    @pl.when(t == nt)
    def _():
        mix()


def _inproj_mix(x3, g, w_bf, layer, cwa, cwb, cbb, wg, bia, bii, ap, mba, mbb, ts):
    B, S, D = x3.shape
    gw = MIX_SLICES * LANES
    nq = D // gw
    nt = S // ts
    wq = N_MIX * gw

    def chan(rows):
        return pl.BlockSpec((rows, gw), lambda b, t, q: (0, q))

    sl = MIX_SLICES
    return pl.pallas_call(
        functools.partial(_inproj_mix_kernel, ts=ts, nt=nt),
        grid=(B, nt + 1, nq),
        in_specs=[pl.BlockSpec((None, ts, D), lambda b, t, q: (b, jnp.minimum(t, nt - 1), 0)),
                  pl.BlockSpec((1, D), lambda b, t, q: (0, 0))]
        + [pl.BlockSpec((None, D, gw), lambda b, t, q, grp=grp: (layer, 0, grp * nq + q)) for grp in range(N_MIX)]
        + [chan(cwa.shape[0]), chan(cwb.shape[0]), chan(1),
                  pl.BlockSpec((sl, LANES, 2 * LANES), lambda b, t, q: (q, 0, 0)),
                  chan(1), chan(1), chan(1), chan(1), chan(1)],
        out_specs=pl.BlockSpec((None, ts, gw),
                               lambda b, t, q: (b, jnp.maximum(t - 1, 0), jnp.where(t == 0, 0, q))),
        out_shape=jax.ShapeDtypeStruct((B, S, D), BF16),
        scratch_shapes=[pltpu.VMEM((ts, D), BF16), pltpu.VMEM((nq, ts, wq), BF16),
                        pltpu.VMEM((nq, sl, SUBLANES, LANES), F32), pltpu.VMEM((nq, sl, SUBLANES, LANES), F32),
                        pltpu.VMEM((nq, sl, 1, LANES), F32),
                        pltpu.VMEM((sl, ts, LANES), F32), pltpu.VMEM((sl, ts, LANES), F32),
                        pltpu.VMEM((sl, ts // SUBLANES, LANES), F32)],
        compiler_params=pltpu.CompilerParams(
            dimension_semantics=("parallel", "arbitrary", "arbitrary"), vmem_limit_bytes=VMEM_LIMIT),
        name="inproj_mix",
    )(x3, g, *([w_bf] * N_MIX), cwa, cwb, cbb, wg, bia, bii, ap, mba, mbb)


def _bmm(a, b):
    return jnp.einsum('nik,nkj->nij', a.astype(BF16), b.astype(BF16), preferred_element_type=F32)


def _bmm_nt(a, b):
    return jnp.einsum('nik,njk->nij', a.astype(BF16), b.astype(BF16), preferred_element_type=F32)


def _bmm_tn(a, b):
    return jnp.einsum('nti,ntj->nij', a.astype(BF16), b.astype(BF16), preferred_element_type=F32)


def _bd(x, bdmask):
    xb = x.astype(BF16)
    return jnp.where(bdmask, jnp.concatenate([xb, xb], axis=1), jnp.zeros((), BF16))


def _tri_inv(A, row, hcol, bdmask):
    def blk(s):
        return (row // s) == (hcol // s)

    def pmm(x, y):
        return _bmm(x, _bd(y, bdmask))

    L = A.shape[1]
    eye = (row == hcol).astype(F32)
    Ad = jnp.where(blk(SUBLANES), A, 0.0)
    T = eye + Ad
    P = Ad
    Pbd = _bd(P, bdmask)
    s = 2
    while s < SUBLANES:
        P = _bmm(P, Pbd)
        Pbd = _bd(P, bdmask)
        T = T + _bmm(T, Pbd)
        s *= 2
    s = SUBLANES
    while s < L:
        Aoff = jnp.where(blk(2 * s) & jnp.logical_not(blk(s)), A, 0.0)
        slabs = [T[:, i * s:(i + 1) * s] for i in range(L // s)]
        low = jnp.concatenate(slabs[1::2], axis=1)
        low = low + pmm(pmm(low, Aoff), T)
        for idx, i in enumerate(range(1, L // s, 2)):
            slabs[i] = low[:, idx * s:(idx + 1) * s]
        T = jnp.concatenate(slabs, axis=1)
        s *= 2
    return T


def _rwkv_kernel(*refs, ts, hg, layer0):
    it = iter(refs)
    x_ref, g1_ref, wgc_ref, wr_ref, wk_ref, wv_ref, ws_ref, mab_ref = [next(it) for _ in range(8)]
    if not layer0:
        v1_ref, vf_ref = next(it), next(it)
    (mur_ref, muk_ref, muv_ref, muwa_ref, mug_ref, w0_ref, a0_ref, kk_ref, ka_ref, rk_ref,
     lng_ref, lnb_ref, mbc_ref) = [next(it) for _ in range(13)]
    if not layer0:
        v0_ref = next(it)
    w2_ref, a2_ref, g2_ref = next(it), next(it), next(it)
    if not layer0:
        v2_ref = next(it)
    o_ref = next(it)
    if layer0:
        vfo_ref = next(it)
    rpad, kpad, vpad, wapad, xgpad, st_scr = [next(it) for _ in range(6)]

    t = pl.program_id(2)
    cw = hg * HEAD
    nc = ts // CHUNK
    npair = hg // 2
    L = CHUNK

    @pl.when(t == 0)
    def _():
        for pad in (rpad, kpad, vpad, wapad, xgpad):
            pad[...] = jnp.zeros_like(pad)
        st_scr[...] = jnp.zeros_like(st_scr)

    hb = _rms(x_ref[...], g1_ref[...]).astype(BF16)
    p_small = _dot(hb, ws_ref[...])

    def token_shift(pad, x, mu_ref):
        (prev,) = _shifted(pad, x, 1, ts)
        return x + (prev - x) * mu_ref[...]

    r = token_shift(rpad, _dot(hb, wr_ref[...]), mur_ref)
    k = token_shift(kpad, _dot(hb, wk_ref[...]), muk_ref)
    v = token_shift(vpad, _dot(hb, wv_ref[...]), muv_ref)
    xwa = token_shift(wapad, p_small[:, :LANES], muwa_ref)
    xg = token_shift(xgpad, p_small[:, LANES:], mug_ref)

    wl = w0_ref[...] + _dot(jnp.tanh(xwa).astype(BF16), w2_ref[...])
    ld = (-LOG2E * EXP_M05) * _sigmoid(wl)
    a = _sigmoid(a0_ref[...] + _dot(xwa.astype(BF16), a2_ref[...]))
    g = _dot(_sigmoid(xg).astype(BF16), g2_ref[...])
    if layer0:
        vfo_ref[...] = v.astype(vfo_ref.dtype)
    else:
        hv = _dot(hb, v1_ref[...])
        mix = _sigmoid(v0_ref[...] + _dot(hv.astype(BF16), v2_ref[...]))
        v = v + (vf_ref[...].astype(F32) - v) * mix

    ri = lax.broadcasted_iota(jnp.int32, (ts, ts), 0)
    ci = lax.broadcasted_iota(jnp.int32, (ts, ts), 1)
    tri = jnp.where(((ri // L) == (ci // L)) & (ci <= ri), 1.0, 0.0).astype(BF16)

    c = _sum_left(tri, ld)
    e_c = jnp.exp2(c)
    e_n = 1.0 / e_c
    e_cl = jnp.broadcast_to(e_c.reshape(nc, L, cw)[:, L - 1:L, :], (nc, L, cw)).reshape(ts, cw)

    kk = k * kk_ref[...]
    kk = kk * lax.rsqrt(jnp.maximum(_head_sums(kk * kk), 1e-24))
    k = k * (1.0 + (a - 1.0) * ka_ref[...])
    bt = kk * a * e_n
    kt = k * e_n
    Rt_f = (r * e_c).astype(BF16)
    At_f = (-kk * jnp.exp2(c - ld)).astype(BF16)
    Bt_f = bt.astype(BF16)
    Kt_f = kt.astype(BF16)
    Bh_f = (bt * e_cl).astype(BF16)
    Kh_f = (kt * e_cl).astype(BF16)
    v_b = v.astype(BF16)

    PW = 2 * HEAD
    prow = lax.broadcasted_iota(jnp.int32, (L, PW), 0)
    hcol = lax.broadcasted_iota(jnp.int32, (L, PW), 1) % HEAD
    low_incl = hcol <= prow
    low_strict = hcol < prow
    brow = lax.broadcasted_iota(jnp.int32, (2 * L, PW), 0)
    bcol = lax.broadcasted_iota(jnp.int32, (2 * L, PW), 1)
    bdmask = (brow // L) == (bcol // HEAD)
    eye2 = brow == bcol

    def pairs(x):
        per_pair = [x[:, p * PW:(p + 1) * PW].reshape(nc, 1, L, PW) for p in range(npair)]
        return jnp.concatenate(per_pair, axis=1).reshape(nc * npair, L, PW)

    Rt, At, Bt, Kt, Bh, Kh, vp = [pairs(x) for x in (Rt_f, At_f, Bt_f, Kt_f, Bh_f, Kh_f, v_b)]
    zero_p = jnp.zeros_like(vp)
    AA = _bmm_nt(jnp.concatenate([Rt, At], axis=1),
                 jnp.concatenate([_bd(Bt, bdmask), _bd(Kt, bdmask)], axis=1))
    Arb = jnp.where(low_incl, AA[:, :L, :PW], 0.0)
    Ark = jnp.where(low_incl, AA[:, :L, PW:], 0.0)
    Aab = jnp.where(low_strict, AA[:, L:, :PW], 0.0)
    Aak = jnp.where(low_strict, AA[:, L:, PW:], 0.0)
    T = _tri_inv(Aab, prow, hcol, bdmask)
    AakV = _bmm(Aak, _bd(vp, bdmask))
    WU = _bmm(T, jnp.concatenate([_bd(At, bdmask), _bd(AakV, bdmask)], axis=2))
    W, U0 = WU[:, :, :PW], WU[:, :, PW:]
    QY = _bmm(jnp.concatenate([Arb, Ark], axis=2),
              jnp.concatenate([jnp.concatenate([_bd(W, bdmask), _bd(U0, bdmask)], axis=2),
                               jnp.concatenate([jnp.zeros((nc * npair, 2 * L, PW), BF16), _bd(vp, bdmask)],
                                               axis=2)], axis=1))
    Q = Rt.astype(F32) + QY[:, :, :PW]
    Y0 = QY[:, :, PW:]
    MG = _bmm_tn(jnp.concatenate([Bh, Kh], axis=1),
                 jnp.concatenate([WU.astype(BF16), jnp.concatenate([zero_p, vp], axis=2)], axis=1))
    pl_row = pairs(e_c)[:, L - 1:L, :]
    M = jnp.where(bdmask, MG[:, :, :PW], 0.0) + jnp.where(eye2, pl_row, 0.0)
    G = jnp.where(bdmask, MG[:, :, PW:], 0.0)
    QM = jnp.concatenate([Q, M], axis=1).astype(BF16)

    H = st_scr[...]
    ys = []
    for ch in range(nc):
        sl = slice(ch * npair, (ch + 1) * npair)
        YH = _bmm(QM[sl], H)
        ys.append(YH[:, :L, :] + Y0[sl])
        H = YH[:, L:, :] + G[sl]
    st_scr[...] = H
    y = jnp.concatenate(
        [jnp.concatenate([ys[ch][p] for ch in range(nc)], axis=0) for p in range(npair)], axis=-1)
    inv_n = 1.0 / HEAD
    yc = y - _head_sums(y) * inv_n
    var = _head_sums(yc * yc) * inv_n
    yn = yc * lax.rsqrt(var + LNX_EPS)
    bonus = _head_sums(r * k * rk_ref[...])
    y_c = (yn * lng_ref[...] + lnb_ref[...] + bonus * v) * g
    m = mab_ref[...].astype(F32) + _sigmoid(_dot(hb, wgc_ref[...]) + mbc_ref[...]) * y_c
    o_ref[...] = m.astype(o_ref.dtype)


def _rwkv(x3, g1, w_bf, layer, col_c, v1p, vf3, mab3, mu, w0, a0, kkp, kap, rkp, lng, lnb, mbc, v0,
          w2p, a2p, g2, v2p, ts, hg):
    B, S, D = x3.shape
    cw = hg * HEAD
    assert cw == D
    layer0 = vf3 is None

    def wcols(c0, w):
        return pl.BlockSpec((None, D, w), lambda b, gi, t: (layer, 0, c0 // w))

    def rowpar(c0, w, grouped):
        return pl.BlockSpec((1, w), lambda b, gi, t: (0, c0 // w + (gi if grouped else 0)))

    def matpar(rows):
        return pl.BlockSpec((rows, cw), lambda b, gi, t: (0, gi))

    act = pl.BlockSpec((None, ts, cw), lambda b, gi, t: (b, t, gi))
    in_specs = [act, pl.BlockSpec((1, D), lambda b, gi, t: (0, 0))]
    in_specs += [wcols(col_c + i * D, D) for i in range(4)] + [wcols(col_c + 4 * D, R_W + R_A + R_G), act]
    args = [x3, g1] + [w_bf] * 5 + [mab3]
    if not layer0:
        in_specs += [pl.BlockSpec((D, LANES), lambda b, gi, t: (0, 0)), act]
        args += [v1p, vf3]
    in_specs += [rowpar(0, cw, True), rowpar(D, cw, True), rowpar(2 * D, cw, True),
                 rowpar(3 * D, LANES, False), rowpar(3 * D + LANES, LANES, False)]
    args += [mu] * 5
    in_specs += [rowpar(0, cw, True)] * 8
    args += [w0, a0, kkp, kap, rkp, lng, lnb, mbc]
    if not layer0:
        in_specs.append(rowpar(0, cw, True))
        args.append(v0)
    in_specs += [matpar(LANES), matpar(LANES), matpar(LANES)]
    args += [w2p, a2p, g2]
    if not layer0:
        in_specs.append(matpar(LANES))
        args.append(v2p)
    out_shape = [jax.ShapeDtypeStruct((B, S, D), BF16)]
    out_specs = [act]
    if layer0:
        out_shape.append(jax.ShapeDtypeStruct((B, S, D), BF16))
        out_specs.append(act)
    res = pl.pallas_call(
        functools.partial(_rwkv_kernel, ts=ts, hg=hg, layer0=layer0),
        grid=(B, D // cw, S // ts),
        in_specs=in_specs, out_specs=out_specs, out_shape=out_shape,
        scratch_shapes=[pltpu.VMEM((SUBLANES, cw), F32)] * 3
        + [pltpu.VMEM((SUBLANES, LANES), F32)] * 2
        + [pltpu.VMEM((hg // 2, 2 * HEAD, 2 * HEAD), F32)],
        compiler_params=pltpu.CompilerParams(
            dimension_semantics=("parallel", "parallel", "arbitrary"), vmem_limit_bytes=VMEM_LIMIT),
        name="rwkv",
    )(*args)
    return (res[0], res[1]) if layer0 else (res[0], vf3)


def _mlp_kernel(x_ref, m_ref, wo_ref, g2_ref, w1_ref, w2_ref, fg_ref, o_ref, acc, h2_scr, *, final):
    j = pl.program_id(1)

    @pl.when(j == 0)
    def _():
        xn = x_ref[...] + _dot(m_ref[...], wo_ref[...])
        acc[...] = xn
        h2_scr[...] = _rms(xn, g2_ref[...]).astype(BF16)

    hid = jnp.square(jnp.maximum(_dot(h2_scr[...], w1_ref[...]), 0.0))
    acc[...] += _dot(hid.astype(BF16), w2_ref[...])

    @pl.when(j == pl.num_programs(1) - 1)
    def _():
        o_ref[...] = _rms(acc[...], fg_ref[...]) if final else acc[...]


def _mlp(x2, m2, wo_bf, g2, w1_bf, w2_bf, layer, fg, tm, tf, final):
    T, D = x2.shape
    FF = w1_bf.shape[-1]
    return pl.pallas_call(
        functools.partial(_mlp_kernel, final=final),
        grid=(T // tm, FF // tf),
        in_specs=[pl.BlockSpec((tm, D), lambda i, j: (i, 0)),
                  pl.BlockSpec((tm, D), lambda i, j: (i, 0)),
                  pl.BlockSpec((None, D, D), lambda i, j: (layer, 0, 0)),
                  pl.BlockSpec((1, D), lambda i, j: (0, 0)),
                  pl.BlockSpec((None, D, tf), lambda i, j: (layer, 0, j)),
                  pl.BlockSpec((None, tf, D), lambda i, j: (layer, j, 0)),
                  pl.BlockSpec((1, D), lambda i, j: (0, 0))],
        out_specs=pl.BlockSpec((tm, D), lambda i, j: (i, 0)),
        out_shape=jax.ShapeDtypeStruct((T, D), F32),
        scratch_shapes=[pltpu.VMEM((tm, D), F32), pltpu.VMEM((tm, D), BF16)],
        compiler_params=pltpu.CompilerParams(
            dimension_semantics=("parallel", "arbitrary"), vmem_limit_bytes=VMEM_LIMIT),
        name="mlp",
    )(x2, m2, wo_bf, g2, w1_bf, w2_bf, fg)


def _tile(n, pref):
    t = min(n, pref)
    while n % t:
        t //= 2
    return t


def kernel(x, norm1_g, w_in, merge_b, conv_a_w, lru_conv_w, lru_conv_b, lru_wa, lru_ba, lru_wi, lru_bi,
           lru_a_param, rwkv_mu, rwkv_w0, rwkv_w2, rwkv_a0, rwkv_a2, rwkv_g2, rwkv_kk, rwkv_ka, rwkv_rk,
           rwkv_lnx_g, rwkv_lnx_b, rwkv_v0, rwkv_v1, rwkv_v2, w_out, norm2_g, mlp_w1, mlp_w2, final_g):
    B, S, D = x.shape
    depth = w_in.shape[0]
    T = B * S
    n_in = w_in.shape[-1]
    assert n_in == 11 * D + R_W + R_A + R_G and R_W + R_A == LANES and R_G == LANES
    assert D % (MIX_SLICES * LANES) == 0

    ts_ab = _tile(S, 1024)
    ts_c = _tile(S, 256)
    hg = 16
    tm_mlp = _tile(T, 1024)
    tf = _tile(mlp_w1.shape[-1], 1024)

    row = lambda vec: vec.reshape(1, -1)
    x2 = x.reshape(T, D)
    vf3 = None
    w_in_bf, w_out_bf = w_in.astype(BF16), w_out.astype(BF16)
    w1_bf, w2_bf = mlp_w1.astype(BF16), mlp_w2.astype(BF16)
    n_mix = N_MIX * D
    for l in range(depth):
        v1p = v2p = None
        if l > 0:
            v1p = jnp.pad(rwkv_v1[l - 1], ((0, 0), (0, LANES - R_V))).astype(BF16)
            v2p = jnp.pad(rwkv_v2[l - 1], ((0, LANES - R_V), (0, 0))).astype(BF16)
        w2p = jnp.pad(rwkv_w2[l], ((0, R_A), (0, 0))).astype(BF16)
        a2p = jnp.pad(rwkv_a2[l], ((R_W, 0), (0, 0))).astype(BF16)
        g2 = rwkv_g2[l].astype(BF16)
        bw = D // LRU_HEADS
        wa = lru_wa[l].reshape(LRU_HEADS // 2, 2, bw, bw)
        wi = lru_wi[l].reshape(LRU_HEADS // 2, 2, bw, bw)
        z = jnp.zeros_like(wa[:, 0])
        blockdiag = lambda w: jnp.concatenate(
            [jnp.concatenate([w[:, 0], z], axis=-1), jnp.concatenate([z, w[:, 1]], axis=-1)], axis=-2)
        wg = jnp.concatenate([blockdiag(wa), blockdiag(wi)], axis=-1).astype(BF16)
        mb = merge_b[l]

        mab3 = _inproj_mix(x2.reshape(B, S, D), row(norm1_g[l]), w_in_bf, l, conv_a_w[l], lru_conv_w[l],
                           row(lru_conv_b[l]), wg, row(lru_ba[l]), row(lru_bi[l]), row(lru_a_param[l]),
                           row(mb[:D]), row(mb[D:2 * D]), ts_ab)
        m3, vf3 = _rwkv(x2.reshape(B, S, D), row(norm1_g[l]), w_in_bf, l, n_mix, v1p, vf3, mab3,
                        row(rwkv_mu[l]), row(rwkv_w0[l]), row(rwkv_a0[l]), row(rwkv_kk[l]), row(rwkv_ka[l]),
                        row(rwkv_rk[l]), row(rwkv_lnx_g[l]), row(rwkv_lnx_b[l]), row(mb[2 * D:]),
                        None if l == 0 else row(rwkv_v0[l - 1]), w2p, a2p, g2, v2p, ts_c, hg)
        x2 = _mlp(x2, m3.reshape(T, D), w_out_bf, row(norm2_g[l]), w1_bf, w2_bf, l, row(final_g),
                  tm_mlp, tf, final=(l == depth - 1))
    return x2.reshape(B, S, D)
```

```python
import functools

import jax
import jax.numpy as jnp
from jax import lax
from jax.experimental import pallas as pl
from jax.experimental.pallas import tpu as pltpu

F32 = jnp.float32
BF16 = jnp.bfloat16

EPS = 1e-6
LRU_C = 8.0
LRU_HEADS = 16
HEAD = 64
CHUNK = 64
LNX_EPS = HEAD * 1e-5
EXP_M05 = 0.6065306597126334
LOG2E = 1.4426950408889634
R_W, R_A, R_G, R_V = 64, 64, 128, 32
LANES = 128
SUBLANES = 8
VMEM_LIMIT = 48 * 1024 * 1024


def _dot(a, b):
    return jnp.dot(a, b, preferred_element_type=F32)


def _sigmoid(x):
    return 1.0 / (1.0 + jnp.exp(-x))


def _softplus(x):
    return jnp.maximum(x, 0.0) + jnp.log1p(jnp.exp(-jnp.abs(x)))


def _rms(x, g):
    return x * lax.rsqrt(jnp.mean(x * x, axis=-1, keepdims=True) + EPS) * g


def _head_sums(x):
    lo = lax.broadcasted_iota(jnp.int32, (x.shape[0], LANES), 1) < HEAD
    cols = []
    for i in range(0, x.shape[1], LANES):
        xi = x[:, i:i + LANES]
        s_lo = jnp.sum(jnp.where(lo, xi, 0.0), axis=-1, keepdims=True)
        s_hi = jnp.sum(xi, axis=-1, keepdims=True) - s_lo
        cols.append(jnp.where(lo, s_lo, s_hi))
    return cols[0] if len(cols) == 1 else jnp.concatenate(cols, axis=1)


def _sum_left(ones_bf, x):
    p1 = x.astype(BF16)
    p2 = (x - p1.astype(F32)).astype(BF16)
    return _dot(ones_bf, p1) + _dot(ones_bf, p2)


def _shifted(tail_ref, x, j, ts):
    w = x.shape[-1]
    x3 = x.reshape(ts // SUBLANES, SUBLANES, w)
    tail3 = tail_ref[...].reshape(1, SUBLANES, w)
    sub = lax.broadcasted_iota(jnp.int32, (1, SUBLANES, w), 1)
    out = []
    for d in range(1, j + 1):
        rot = pltpu.roll(x3, d, 1)
        prev_group = jnp.concatenate([pltpu.roll(tail3, d, 1), rot[:-1]], axis=0)
        out.append(jnp.where(sub < d, prev_group, rot).reshape(ts, w))
    tail_ref[...] = x[ts - SUBLANES:, :]
    return out


N_MIX = 7
MIX_SLICES = 2


def _mix_slice(cols, first_tile, cwa, cwb, cbb, wg, bia, bii, ap, mba, mbb,
               ztail, xbtail, hcar, hs_scr, as_scr, ent_scr, ts):
    ba, ca, xa, xb, gb, ga, gbb = [c.astype(F32) for c in cols]

    z = ca * xa
    z1, z2 = _shifted(ztail, z, 2, ts)
    y_a = ba * (z * cwa[2:3] + z1 * cwa[1:2] + z2 * cwa[0:1])

    x1, x2, x3 = _shifted(xbtail, xb, 3, ts)
    u = xb * cwb[3:4] + x1 * cwb[2:3] + x2 * cwb[1:2] + x3 * cwb[0:1] + cbb

    res = _dot(u.astype(BF16), wg)
    gate_a = _sigmoid(res[:, :LANES] + bia)
    gate_i = _sigmoid(res[:, LANES:] + bii)
    log_a = (-LRU_C) * gate_a * _softplus(ap)
    a = jnp.exp(log_a)
    m2 = -jnp.tanh(log_a) * (1.0 + a * a)
    mult = jnp.where(m2 > 0.0, m2 * lax.rsqrt(m2), 0.0)
    row = lax.broadcasted_iota(jnp.int32, (ts, LANES), 0)
    mult = jnp.where((row == 0) & first_tile, 1.0, mult)
    h = u * gate_i * mult

    ng = ts // SUBLANES
    hs_scr[...] = h
    as_scr[...] = a
    cum_a, loc_h = [], []
    for s in range(SUBLANES):
        a_s = as_scr[pl.ds(s, ng, stride=SUBLANES), :]
        u_s = hs_scr[pl.ds(s, ng, stride=SUBLANES), :]
        cum_a.append(a_s * cum_a[-1] if s else a_s)
        loc_h.append(a_s * loc_h[-1] + u_s if s else u_s)
    carry = hcar[...]
    for gi in range(ng):
        ent_scr[gi:gi + 1, :] = carry
        carry = loc_h[-1][gi:gi + 1] + cum_a[-1][gi:gi + 1] * carry
    hcar[...] = carry
    enter = ent_scr[...]
    for s in range(SUBLANES):
        hs_scr[pl.ds(s, ng, stride=SUBLANES), :] = loc_h[s] + cum_a[s] * enter
    h = hs_scr[...]

    gelu = 0.5 * gb * (1.0 + jnp.tanh(0.7978845608028654 * (gb + 0.044715 * gb * gb * gb)))
    return _sigmoid(ga + mba) * y_a + _sigmoid(gbb + mbb) * (h * gelu)


def _inproj_mix_kernel(*refs, ts, nt):
    x_ref, g_ref = refs[:2]
    w_refs = refs[2:2 + N_MIX]
    (cwa_ref, cwb_ref, cbb_ref, wg_ref, bia_ref, bii_ref, ap_ref, mba_ref, mbb_ref, o_ref,
     h_scr, pq_scr, ztail, xbtail, hcar, hs_scr, as_scr, ent_scr) = refs[2 + N_MIX:]
    t = pl.program_id(1)
    q = pl.program_id(2)
    gw = MIX_SLICES * LANES

    @pl.when((q == 0) & (t < nt))
    def _():
        h_scr[...] = _rms(x_ref[...], g_ref[...]).astype(BF16)

    @pl.when(t == 1)
    def _():
        ztail[q] = jnp.zeros(ztail.shape[1:], F32)
        xbtail[q] = jnp.zeros(xbtail.shape[1:], F32)
        hcar[q] = jnp.zeros(hcar.shape[1:], F32)

    def mix():
        pq = pq_scr[q]
        for i in range(MIX_SLICES):
            lanes = slice(i * LANES, (i + 1) * LANES)
            cols = [pq[:, g * gw + i * LANES:g * gw + (i + 1) * LANES] for g in range(N_MIX)]
            o_ref[:, lanes] = _mix_slice(
                cols, t == 1, cwa_ref[:, lanes], cwb_ref[:, lanes], cbb_ref[:, lanes],
                wg_ref[i], bia_ref[:, lanes], bii_ref[:, lanes], ap_ref[:, lanes], mba_ref[:, lanes],
                mbb_ref[:, lanes], ztail.at[q, i], xbtail.at[q, i], hcar.at[q, i],
                hs_scr.at[i], as_scr.at[i], ent_scr.at[i], ts)

    def project():
        hb = h_scr[...]
        for g in range(N_MIX):
            pq_scr[q, :, g * gw:(g + 1) * gw] = _dot(hb, w_refs[g][...]).astype(BF16)

    @pl.when(t == 0)
    def _():
        o_ref[...] = jnp.zeros_like(o_ref)
        project()

    @pl.when((t > 0) & (t < nt))
    def _():
        mix()
        project()

    @pl.when(t == nt)
    def _():
        mix()


def _inproj_mix(x3, g, w_bf, layer, cwa, cwb, cbb, wg, bia, bii, ap, mba, mbb, ts):
    B, S, D = x3.shape
    gw = MIX_SLICES * LANES
    nq = D // gw
    nt = S // ts
    wq = N_MIX * gw

    def chan(rows):
        return pl.BlockSpec((rows, gw), lambda b, t, q: (0, q))

    sl = MIX_SLICES
    return pl.pallas_call(
        functools.partial(_inproj_mix_kernel, ts=ts, nt=nt),
        grid=(B, nt + 1, nq),
        in_specs=[pl.BlockSpec((None, ts, D), lambda b, t, q: (b, jnp.minimum(t, nt - 1), 0)),
                  pl.BlockSpec((1, D), lambda b, t, q: (0, 0))]
        + [pl.BlockSpec((None, D, gw), lambda b, t, q, grp=grp: (layer, 0, grp * nq + q)) for grp in range(N_MIX)]
        + [chan(cwa.shape[0]), chan(cwb.shape[0]), chan(1),
                  pl.BlockSpec((sl, LANES, 2 * LANES), lambda b, t, q: (q, 0, 0)),
                  chan(1), chan(1), chan(1), chan(1), chan(1)],
        out_specs=pl.BlockSpec((None, ts, gw),
                               lambda b, t, q: (b, jnp.maximum(t - 1, 0), jnp.where(t == 0, 0, q))),
        out_shape=jax.ShapeDtypeStruct((B, S, D), F32),
        scratch_shapes=[pltpu.VMEM((ts, D), BF16), pltpu.VMEM((nq, ts, wq), BF16),
                        pltpu.VMEM((nq, sl, SUBLANES, LANES), F32), pltpu.VMEM((nq, sl, SUBLANES, LANES), F32),
                        pltpu.VMEM((nq, sl, 1, LANES), F32),
                        pltpu.VMEM((sl, ts, LANES), F32), pltpu.VMEM((sl, ts, LANES), F32),
                        pltpu.VMEM((sl, ts // SUBLANES, LANES), F32)],
        compiler_params=pltpu.CompilerParams(
            dimension_semantics=("parallel", "arbitrary", "arbitrary"), vmem_limit_bytes=VMEM_LIMIT),
        name="inproj_mix",
    )(x3, g, *([w_bf] * N_MIX), cwa, cwb, cbb, wg, bia, bii, ap, mba, mbb)


def _bmm(a, b):
    return jnp.einsum('nik,nkj->nij', a.astype(BF16), b.astype(BF16), preferred_element_type=F32)


def _bmm_nt(a, b):
    return jnp.einsum('nik,njk->nij', a.astype(BF16), b.astype(BF16), preferred_element_type=F32)


def _bmm_tn(a, b):
    return jnp.einsum('nti,ntj->nij', a.astype(BF16), b.astype(BF16), preferred_element_type=F32)


def _bd(x, bdmask):
    xb = x.astype(BF16)
    return jnp.where(bdmask, jnp.concatenate([xb, xb], axis=1), jnp.zeros((), BF16))


def _tri_inv(A, row, hcol, bdmask):
    def blk(s):
        return (row // s) == (hcol // s)

    def pmm(x, y):
        return _bmm(x, _bd(y, bdmask))

    L = A.shape[1]
    eye = (row == hcol).astype(F32)
    Ad = jnp.where(blk(SUBLANES), A, 0.0)
    T = eye + Ad
    P = Ad
    Pbd = _bd(P, bdmask)
    s = 2
    while s < SUBLANES:
        P = _bmm(P, Pbd)
        Pbd = _bd(P, bdmask)
        T = T + _bmm(T, Pbd)
        s *= 2
    s = SUBLANES
    while s < L:
        Aoff = jnp.where(blk(2 * s) & jnp.logical_not(blk(s)), A, 0.0)
        slabs = [T[:, i * s:(i + 1) * s] for i in range(L // s)]
        low = jnp.concatenate(slabs[1::2], axis=1)
        low = low + pmm(pmm(low, Aoff), T)
        for idx, i in enumerate(range(1, L // s, 2)):
            slabs[i] = low[:, idx * s:(idx + 1) * s]
        T = jnp.concatenate(slabs, axis=1)
        s *= 2
    return T


def _rwkv_kernel(*refs, ts, hg, layer0):
    it = iter(refs)
    x_ref, g1_ref, wgc_ref, wr_ref, wk_ref, wv_ref, ws_ref, mab_ref = [next(it) for _ in range(8)]
    if not layer0:
        v1_ref, vf_ref = next(it), next(it)
    (mur_ref, muk_ref, muv_ref, muwa_ref, mug_ref, w0_ref, a0_ref, kk_ref, ka_ref, rk_ref,
     lng_ref, lnb_ref, mbc_ref) = [next(it) for _ in range(13)]
    if not layer0:
        v0_ref = next(it)
    w2_ref, a2_ref, g2_ref = next(it), next(it), next(it)
    if not layer0:
        v2_ref = next(it)
    o_ref = next(it)
    if layer0:
        vfo_ref = next(it)
    rpad, kpad, vpad, wapad, xgpad, st_scr = [next(it) for _ in range(6)]

    t = pl.program_id(2)
    cw = hg * HEAD
    nc = ts // CHUNK
    npair = hg // 2
    L = CHUNK

    @pl.when(t == 0)
    def _():
        for pad in (rpad, kpad, vpad, wapad, xgpad):
            pad[...] = jnp.zeros_like(pad)
        st_scr[...] = jnp.zeros_like(st_scr)

    ri = lax.broadcasted_iota(jnp.int32, (ts, ts), 0)
    ci = lax.broadcasted_iota(jnp.int32, (ts, ts), 1)

    hb = _rms(x_ref[...], g1_ref[...]).astype(BF16)
    p_small = _dot(hb, ws_ref[...])

    def token_shift(pad, x, mu_ref):
        (prev,) = _shifted(pad, x, 1, ts)
        return x + (prev - x) * mu_ref[...]

    r = token_shift(rpad, _dot(hb, wr_ref[...]), mur_ref)
    k = token_shift(kpad, _dot(hb, wk_ref[...]), muk_ref)
    v = token_shift(vpad, _dot(hb, wv_ref[...]), muv_ref)
    xwa = token_shift(wapad, p_small[:, :LANES], muwa_ref)
    xg = token_shift(xgpad, p_small[:, LANES:], mug_ref)

    wl = w0_ref[...] + _dot(jnp.tanh(xwa).astype(BF16), w2_ref[...])
    ld = (-LOG2E * EXP_M05) * _sigmoid(wl)
    a = _sigmoid(a0_ref[...] + _dot(xwa.astype(BF16), a2_ref[...]))
    g = _dot(_sigmoid(xg).astype(BF16), g2_ref[...])
    if layer0:
        vfo_ref[...] = v
    else:
        hv = _dot(hb, v1_ref[...])
        mix = _sigmoid(v0_ref[...] + _dot(hv.astype(BF16), v2_ref[...]))
        v = v + (vf_ref[...] - v) * mix

    ri = lax.broadcasted_iota(jnp.int32, (ts, ts), 0)
    ci = lax.broadcasted_iota(jnp.int32, (ts, ts), 1)
    tri = jnp.where(((ri // L) == (ci // L)) & (ci <= ri), 1.0, 0.0).astype(BF16)

    c = _sum_left(tri, ld)
    e_c = jnp.exp2(c)
    e_n = 1.0 / e_c
    e_cl = jnp.broadcast_to(e_c.reshape(nc, L, cw)[:, L - 1:L, :], (nc, L, cw)).reshape(ts, cw)

    kk = k * kk_ref[...]
    kk = kk * lax.rsqrt(jnp.maximum(_head_sums(kk * kk), 1e-24))
    k = k * (1.0 + (a - 1.0) * ka_ref[...])
    bt = kk * a * e_n
    kt = k * e_n
    Rt_f = (r * e_c).astype(BF16)
    At_f = (-kk * jnp.exp2(c - ld)).astype(BF16)
    Bt_f = bt.astype(BF16)
    Kt_f = kt.astype(BF16)
    Bh_f = (bt * e_cl).astype(BF16)
    Kh_f = (kt * e_cl).astype(BF16)
    v_b = v.astype(BF16)

    PW = 2 * HEAD
    prow = lax.broadcasted_iota(jnp.int32, (L, PW), 0)
    hcol = lax.broadcasted_iota(jnp.int32, (L, PW), 1) % HEAD
    low_incl = hcol <= prow
    low_strict = hcol < prow
    brow = lax.broadcasted_iota(jnp.int32, (2 * L, PW), 0)
    bcol = lax.broadcasted_iota(jnp.int32, (2 * L, PW), 1)
    bdmask = (brow // L) == (bcol // HEAD)
    eye2 = brow == bcol

    def pairs(x):
        per_pair = [x[:, p * PW:(p + 1) * PW].reshape(nc, 1, L, PW) for p in range(npair)]
        return jnp.concatenate(per_pair, axis=1).reshape(nc * npair, L, PW)

    Rt, At, Bt, Kt, Bh, Kh, vp = [pairs(x) for x in (Rt_f, At_f, Bt_f, Kt_f, Bh_f, Kh_f, v_b)]
    zero_p = jnp.zeros_like(vp)
    AA = _bmm_nt(jnp.concatenate([Rt, At], axis=1),
                 jnp.concatenate([_bd(Bt, bdmask), _bd(Kt, bdmask)], axis=1))
    Arb = jnp.where(low_incl, AA[:, :L, :PW], 0.0)
    Ark = jnp.where(low_incl, AA[:, :L, PW:], 0.0)
    Aab = jnp.where(low_strict, AA[:, L:, :PW], 0.0)
    Aak = jnp.where(low_strict, AA[:, L:, PW:], 0.0)
    T = _tri_inv(Aab, prow, hcol, bdmask)
    AakV = _bmm(Aak, _bd(vp, bdmask))
    WU = _bmm(T, jnp.concatenate([_bd(At, bdmask), _bd(AakV, bdmask)], axis=2))
    W, U0 = WU[:, :, :PW], WU[:, :, PW:]
    QY = _bmm(jnp.concatenate([Arb, Ark], axis=2),
              jnp.concatenate([jnp.concatenate([_bd(W, bdmask), _bd(U0, bdmask)], axis=2),
                               jnp.concatenate([jnp.zeros((nc * npair, 2 * L, PW), BF16), _bd(vp, bdmask)],
                                               axis=2)], axis=1))
    Q = Rt.astype(F32) + QY[:, :, :PW]
    Y0 = QY[:, :, PW:]
    MG = _bmm_tn(jnp.concatenate([Bh, Kh], axis=1),
                 jnp.concatenate([WU.astype(BF16), jnp.concatenate([zero_p, vp], axis=2)], axis=1))
    pl_row = pairs(e_c)[:, L - 1:L, :]
    M = jnp.where(bdmask, MG[:, :, :PW], 0.0) + jnp.where(eye2, pl_row, 0.0)
    G = jnp.where(bdmask, MG[:, :, PW:], 0.0)
    QM = jnp.concatenate([Q, M], axis=1).astype(BF16)

    H = st_scr[...]
    ys = []
    for ch in range(nc):
        sl = slice(ch * npair, (ch + 1) * npair)
        YH = _bmm(QM[sl], H)
        ys.append(YH[:, :L, :] + Y0[sl])
        H = YH[:, L:, :] + G[sl]
    st_scr[...] = H
    y = jnp.concatenate(
        [jnp.concatenate([ys[ch][p] for ch in range(nc)], axis=0) for p in range(npair)], axis=-1)
    inv_n = 1.0 / HEAD
    yc = y - _head_sums(y) * inv_n
    var = _head_sums(yc * yc) * inv_n
    yn = yc * lax.rsqrt(var + LNX_EPS)
    bonus = _head_sums(r * k * rk_ref[...])
    y_c = (yn * lng_ref[...] + lnb_ref[...] + bonus * v) * g
    m = mab_ref[...] + _sigmoid(_dot(hb, wgc_ref[...]) + mbc_ref[...]) * y_c
    o_ref[...] = m.astype(o_ref.dtype)


def _rwkv(x3, g1, w_bf, layer, col_c, v1p, vf3, mab3, mu, w0, a0, kkp, kap, rkp, lng, lnb, mbc, v0,
          w2p, a2p, g2, v2p, ts, hg):
    B, S, D = x3.shape
    cw = hg * HEAD
    assert cw == D
    layer0 = vf3 is None

    def wcols(c0, w):
        return pl.BlockSpec((None, D, w), lambda b, gi, t: (layer, 0, c0 // w))

    def rowpar(c0, w, grouped):
        return pl.BlockSpec((1, w), lambda b, gi, t: (0, c0 // w + (gi if grouped else 0)))

    def matpar(rows):
        return pl.BlockSpec((rows, cw), lambda b, gi, t: (0, gi))

    act = pl.BlockSpec((None, ts, cw), lambda b, gi, t: (b, t, gi))
    in_specs = [act, pl.BlockSpec((1, D), lambda b, gi, t: (0, 0))]
    in_specs += [wcols(col_c + i * D, D) for i in range(4)] + [wcols(col_c + 4 * D, R_W + R_A + R_G), act]
    args = [x3, g1] + [w_bf] * 5 + [mab3]
    if not layer0:
        in_specs += [pl.BlockSpec((D, LANES), lambda b, gi, t: (0, 0)), act]
        args += [v1p, vf3]
    in_specs += [rowpar(0, cw, True), rowpar(D, cw, True), rowpar(2 * D, cw, True),
                 rowpar(3 * D, LANES, False), rowpar(3 * D + LANES, LANES, False)]
    args += [mu] * 5
    in_specs += [rowpar(0, cw, True)] * 8
    args += [w0, a0, kkp, kap, rkp, lng, lnb, mbc]
    if not layer0:
        in_specs.append(rowpar(0, cw, True))
        args.append(v0)
    in_specs += [matpar(LANES), matpar(LANES), matpar(LANES)]
    args += [w2p, a2p, g2]
    if not layer0:
        in_specs.append(matpar(LANES))
        args.append(v2p)
    out_shape = [jax.ShapeDtypeStruct((B, S, D), BF16)]
    out_specs = [act]
    if layer0:
        out_shape.append(jax.ShapeDtypeStruct((B, S, D), F32))
        out_specs.append(act)
    res = pl.pallas_call(
        functools.partial(_rwkv_kernel, ts=ts, hg=hg, layer0=layer0),
        grid=(B, D // cw, S // ts),
        in_specs=in_specs, out_specs=out_specs, out_shape=out_shape,
        scratch_shapes=[pltpu.VMEM((SUBLANES, cw), F32)] * 3
        + [pltpu.VMEM((SUBLANES, LANES), F32)] * 2
        + [pltpu.VMEM((hg // 2, 2 * HEAD, 2 * HEAD), F32)],
        compiler_params=pltpu.CompilerParams(
            dimension_semantics=("parallel", "parallel", "arbitrary"), vmem_limit_bytes=VMEM_LIMIT),
        name="rwkv",
    )(*args)
    return (res[0], res[1]) if layer0 else (res[0], vf3)


def _mlp_kernel(x_ref, m_ref, wo_ref, g2_ref, w1_ref, w2_ref, fg_ref, o_ref, acc, h2_scr, hid_scr, *, final, nf):
    j = pl.program_id(1)

    def up():
        hid = jnp.square(jnp.maximum(_dot(h2_scr[...], w1_ref[...]), 0.0))
        hid_scr[j % 2] = hid.astype(BF16)

    def down():
        acc[...] += _dot(hid_scr[(j + 1) % 2], w2_ref[...])

    @pl.when(j == 0)
    def _():
        xn = x_ref[...] + _dot(m_ref[...], wo_ref[...])
        acc[...] = xn
        h2_scr[...] = _rms(xn, g2_ref[...]).astype(BF16)
        up()

    @pl.when((j > 0) & (j < nf))
    def _():
        down()
        up()

    @pl.when(j == nf)
    def _():
        down()
        o_ref[...] = _rms(acc[...], fg_ref[...]) if final else acc[...]


def _mlp(x2, m2, wo_bf, g2, w1_bf, w2_bf, layer, fg, tm, tf, final):
    T, D = x2.shape
    FF = w1_bf.shape[-1]
    nf = FF // tf
    return pl.pallas_call(
        functools.partial(_mlp_kernel, final=final, nf=nf),
        grid=(T // tm, nf + 1),
        in_specs=[pl.BlockSpec((tm, D), lambda i, j: (i, 0)),
                  pl.BlockSpec((tm, D), lambda i, j: (i, 0)),
                  pl.BlockSpec((None, D, D), lambda i, j: (layer, 0, 0)),
                  pl.BlockSpec((1, D), lambda i, j: (0, 0)),
                  pl.BlockSpec((None, D, tf), lambda i, j: (layer, 0, jnp.minimum(j, nf - 1))),
                  pl.BlockSpec((None, tf, D), lambda i, j: (layer, jnp.maximum(j - 1, 0), 0)),
                  pl.BlockSpec((1, D), lambda i, j: (0, 0))],
        out_specs=pl.BlockSpec((tm, D), lambda i, j: (i, 0)),
        out_shape=jax.ShapeDtypeStruct((T, D), F32),
        scratch_shapes=[pltpu.VMEM((tm, D), F32), pltpu.VMEM((tm, D), BF16), pltpu.VMEM((2, tm, tf), BF16)],
        compiler_params=pltpu.CompilerParams(
            dimension_semantics=("parallel", "arbitrary"), vmem_limit_bytes=VMEM_LIMIT),
        name="mlp",
    )(x2, m2, wo_bf, g2, w1_bf, w2_bf, fg)


def _tile(n, pref):
    t = min(n, pref)
    while n % t:
        t //= 2
    return t


def kernel(x, norm1_g, w_in, merge_b, conv_a_w, lru_conv_w, lru_conv_b, lru_wa, lru_ba, lru_wi, lru_bi,
           lru_a_param, rwkv_mu, rwkv_w0, rwkv_w2, rwkv_a0, rwkv_a2, rwkv_g2, rwkv_kk, rwkv_ka, rwkv_rk,
           rwkv_lnx_g, rwkv_lnx_b, rwkv_v0, rwkv_v1, rwkv_v2, w_out, norm2_g, mlp_w1, mlp_w2, final_g):
    B, S, D = x.shape
    depth = w_in.shape[0]
    T = B * S
    n_in = w_in.shape[-1]
    assert n_in == 11 * D + R_W + R_A + R_G and R_W + R_A == LANES and R_G == LANES
    assert D % (MIX_SLICES * LANES) == 0

    ts_ab = _tile(S, 1024)
    ts_c = _tile(S, 256)
    hg = 16
    tm_mlp = _tile(T, 1024)
    tf = _tile(mlp_w1.shape[-1], 1024)

    row = lambda vec: vec.reshape(1, -1)
    x2 = x.reshape(T, D)
    vf3 = None
    w_in_bf, w_out_bf = w_in.astype(BF16), w_out.astype(BF16)
    w1_bf, w2_bf = mlp_w1.astype(BF16), mlp_w2.astype(BF16)
    n_mix = N_MIX * D
    for l in range(depth):
        v1p = v2p = None
        if l > 0:
            v1p = jnp.pad(rwkv_v1[l - 1], ((0, 0), (0, LANES - R_V))).astype(BF16)
            v2p = jnp.pad(rwkv_v2[l - 1], ((0, LANES - R_V), (0, 0))).astype(BF16)
        w2p = jnp.pad(rwkv_w2[l], ((0, R_A), (0, 0))).astype(BF16)
        a2p = jnp.pad(rwkv_a2[l], ((R_W, 0), (0, 0))).astype(BF16)
        g2 = rwkv_g2[l].astype(BF16)
        bw = D // LRU_HEADS
        wa = lru_wa[l].reshape(LRU_HEADS // 2, 2, bw, bw)
        wi = lru_wi[l].reshape(LRU_HEADS // 2, 2, bw, bw)
        z = jnp.zeros_like(wa[:, 0])
        blockdiag = lambda w: jnp.concatenate(
            [jnp.concatenate([w[:, 0], z], axis=-1), jnp.concatenate([z, w[:, 1]], axis=-1)], axis=-2)
        wg = jnp.concatenate([blockdiag(wa), blockdiag(wi)], axis=-1).astype(BF16)
        mb = merge_b[l]

        mab3 = _inproj_mix(x2.reshape(B, S, D), row(norm1_g[l]), w_in_bf, l, conv_a_w[l], lru_conv_w[l],
                           row(lru_conv_b[l]), wg, row(lru_ba[l]), row(lru_bi[l]), row(lru_a_param[l]),
                           row(mb[:D]), row(mb[D:2 * D]), ts_ab)
        m3, vf3 = _rwkv(x2.reshape(B, S, D), row(norm1_g[l]), w_in_bf, l, n_mix, v1p, vf3, mab3,
                        row(rwkv_mu[l]), row(rwkv_w0[l]), row(rwkv_a0[l]), row(rwkv_kk[l]), row(rwkv_ka[l]),
                        row(rwkv_rk[l]), row(rwkv_lnx_g[l]), row(rwkv_lnx_b[l]), row(mb[2 * D:]),
                        None if l == 0 else row(rwkv_v0[l - 1]), w2p, a2p, g2, v2p, ts_c, hg)
        x2 = _mlp(x2, m3.reshape(T, D), w_out_bf, row(norm2_g[l]), w1_bf, w2_bf, l, row(final_g),
                  tm_mlp, tf, final=(l == depth - 1))
    return x2.reshape(B, S, D)
```
